```python
import math
import jax
import jax.numpy as jnp
from jax import lax
import numpy as np

D_MODEL = 2048
BATCH = 8
SEQ = 2048
DEPTH = 2
DEC_BATCH = 32
DEC_SEQ = 1
PAST_LEN = 8192
PAGE_SIZE = 128

N_EVEN = (DEPTH + 1) // 2
N_ODD = DEPTH // 2
H_A = 4
D_MLSTM = D_MODEL // 2
DK_A = D_MLSTM // H_A
DV_A = DK_A
MLSTM_CHUNK = 128
D_CONV = D_MODEL - D_MLSTM
CONV_W = 3
IN_EVEN = 4 * D_MLSTM + 2 * H_A + 3 * D_CONV
H_C = 8
DH_C = D_MODEL // (2 * H_C)
DV_C = 2 * DH_C
D_ATTN = H_C * DV_C
Q_BLOCK = 128
D_FF = int(math.ceil(8 * D_MODEL / 3 / 256)) * 256
ALPHA = (2 * DEPTH) ** 0.25
BETA = (8 * DEPTH) ** -0.25
LN_EPS = 1e-5

kernel_name = 'hybrid_mlstm_shortconv_diffattn_decode_step'


def layer_norm(x, g, b):
    xf = x.astype(jnp.float32)
    mu = jnp.mean(xf, axis=-1, keepdims=True)
    var = jnp.mean(jnp.square(xf - mu), axis=-1, keepdims=True)
    y = (xf - mu) * lax.rsqrt(var + LN_EPS) * g.astype(jnp.float32) + b.astype(jnp.float32)
    return y.astype(x.dtype)


def post_norm(x, f, g, b):
    return layer_norm(ALPHA * x + f, g, b)


def swiglu(x, wg, wu, wd):
    return (jax.nn.silu(x @ wg) * (x @ wu)) @ wd


def alibi_slopes():
    return jnp.exp2(-8.0 * jnp.arange(1, H_C + 1, dtype=jnp.float32) / H_C)


def mlstm_chunkwise(q, k, v, ig, lf, c0, n0, m0):
    bsz, t, h, _ = q.shape
    L = math.gcd(t, MLSTM_CHUNK)
    nc = t // L

    def to_chunks(a):
        a = a.astype(jnp.float32).reshape((bsz, nc, L, h) + a.shape[3:])
        return jnp.moveaxis(a, (1, 3), (0, 2))

    causal = jnp.tril(jnp.ones((L, L), dtype=bool))

    def step(carry, inp):
        C, n, m = carry
        qc, kc, vc, igc, lfc = inp
        b = jnp.cumsum(lfc, axis=-1)
        dmat = b[..., :, None] - b[..., None, :] + igc[..., None, :]
        dmat = jnp.where(causal, dmat, -jnp.inf)
        inter = b + m[..., None]
        m_t = jnp.maximum(inter, jnp.max(dmat, axis=-1))
        w_intra = jnp.exp(dmat - m_t[..., None])
        w_inter = jnp.exp(inter - m_t)
        s = jnp.einsum('bhtd,bhsd->bhts', qc, kc) * w_intra
        num = w_inter[..., None] * jnp.einsum('bhtd,bhde->bhte', qc, C) + jnp.einsum('bhts,bhse->bhte', s, vc)
        den = w_inter * jnp.einsum('bhtd,bhd->bht', qc, n) + jnp.sum(s, axis=-1)
        out = num / jnp.maximum(jnp.abs(den), jnp.exp(-m_t))[..., None]
        m_new = m_t[..., -1]
        g_inter = jnp.exp(b[..., -1] + m - m_new)
        g_s = jnp.exp(b[..., -1:] - b + igc - m_new[..., None])
        C_new = g_inter[..., None, None] * C + jnp.einsum('bhs,bhsd,bhse->bhde', g_s, kc, vc)
        n_new = g_inter[..., None] * n + jnp.einsum('bhs,bhsd->bhd', g_s, kc)
        return (C_new, n_new, m_new), out

    xs = (to_chunks(q), to_chunks(k), to_chunks(v), to_chunks(ig), to_chunks(lf))
    carry0 = (c0.astype(jnp.float32), n0.astype(jnp.float32), m0.astype(jnp.float32))
    (c1, n1, m1), hs = lax.scan(step, carry0, xs)
    hs = jnp.moveaxis(hs, (0, 2), (1, 3)).reshape(bsz, t, h, hs.shape[-1])
    return hs, c1, n1, m1


def even_mixer(x, w_in, b_gate, conv_w, w_out, c0, n0, m0, conv0):
    bsz, t, _ = x.shape
    sizes = [D_MLSTM] * 4 + [H_A] * 2 + [D_CONV] * 3
    cuts = np.cumsum(sizes)[:-1].tolist()
    q, k, v, o, ig, fg, gb, gc, hc = jnp.split(x @ w_in, cuts, axis=-1)
    q = q.reshape(bsz, t, H_A, DK_A)
    k = k.reshape(bsz, t, H_A, DK_A) * (DK_A ** -0.5)
    v = v.reshape(bsz, t, H_A, DV_A)
    ig = (ig + b_gate[:H_A]).astype(jnp.float32)
    lf = jax.nn.log_sigmoid((fg + b_gate[H_A:]).astype(jnp.float32))
    hm, c1, n1, m1 = mlstm_chunkwise(q, k, v, ig, lf, c0, n0, m0)
    hm = jax.nn.sigmoid(o) * hm.reshape(bsz, t, D_MLSTM).astype(x.dtype)
    u = gc * hc
    upad = jnp.concatenate([conv0.astype(u.dtype), u], axis=1)
    yc = sum(conv_w[j] * upad[:, j:j + t] for j in range(CONV_W))
    yc = gb * yc
    out = jnp.concatenate([hm, yc], axis=-1) @ w_out
    return out, c1, n1, m1, upad[:, t:]


def diff_qkv(x, w_qkv):
    bsz, t, _ = x.shape
    q, k, v = jnp.split(x @ w_qkv, 3, axis=-1)
    q = q.reshape(bsz, t, H_C, 2, DH_C) * (DH_C ** -0.5)
    k = k.reshape(bsz, t, H_C, 2, DH_C)
    v = v.reshape(bsz, t, H_C, DV_C)
    return q, k, v


def diff_lambda(lq1, lk1, lq2, lk2, lam_init):
    a = jnp.sum(lq1.astype(jnp.float32) * lk1.astype(jnp.float32))
    b = jnp.sum(lq2.astype(jnp.float32) * lk2.astype(jnp.float32))
    return jnp.exp(a) - jnp.exp(b) + lam_init


def diff_core(q, k, v, q_pos, k_pos, lam, slopes):
    s = jnp.einsum('bqhcd,bkhcd->bchqk', q, k).astype(jnp.float32)
    dist = q_pos[:, None] - k_pos[None, :]
    s = s - slopes[:, None, None] * dist.astype(jnp.float32)
    s = jnp.where(dist >= 0, s, -jnp.inf)
    p = jax.nn.softmax(s, axis=-1)
    a = p[:, 0] - lam * p[:, 1]
    return jnp.einsum('bhqk,bkhe->bqhe', a, v.astype(jnp.float32))


def diff_out(o, subln_g, lam_init, w_o, dtype):
    bsz, t = o.shape[:2]
    o = o * lax.rsqrt(jnp.mean(o * o, axis=-1, keepdims=True) + LN_EPS) * subln_g.astype(jnp.float32)
    o = o * (1.0 - lam_init)
    return o.reshape(bsz, t, D_ATTN).astype(dtype) @ w_o


def diff_prompt(x, w_qkv, w_o, subln_g, lam, lam_init, slopes):
    bsz, t, _ = x.shape
    q, k, v = diff_qkv(x, w_qkv)
    nb = t // Q_BLOCK
    qb = jnp.moveaxis(q.reshape(bsz, nb, Q_BLOCK, H_C, 2, DH_C), 1, 0)
    k_pos = jnp.arange(t)

    def block(args):
        qi, i = args
        q_pos = i * Q_BLOCK + jnp.arange(Q_BLOCK)
        return diff_core(qi, k, v, q_pos, k_pos, lam, slopes)

    ob = lax.map(block, (qb, jnp.arange(nb)))
    o = jnp.moveaxis(ob, 0, 1).reshape(bsz, t, H_C, DV_C)
    y = diff_out(o, subln_g, lam_init, w_o, x.dtype)
    return y, k.reshape(bsz, t, H_C, 2 * DH_C), v


def diff_sample(x, cache_k, cache_v, j, page_table, w_qkv, w_o, subln_g, lam, lam_init, slopes):
    bsz, t, _ = x.shape
    q, k, v = diff_qkv(x, w_qkv)
    past = page_table.shape[1] * PAGE_SIZE
    kp = cache_k[j, page_table].reshape(bsz, past, H_C, 2, DH_C)
    vp = cache_v[j, page_table].reshape(bsz, past, H_C, DV_C)
    k_all = jnp.concatenate([kp.astype(k.dtype), k], axis=1)
    v_all = jnp.concatenate([vp.astype(v.dtype), v], axis=1)
    q_pos = past + jnp.arange(t)
    k_pos = jnp.arange(past + t)
    o = diff_core(q, k_all, v_all, q_pos, k_pos, lam, slopes)
    y = diff_out(o, subln_g, lam_init, w_o, x.dtype)
    return y, k.reshape(bsz, t, H_C, 2 * DH_C), v


def setup_inputs(seed: int = 0) -> dict:
    key = jax.random.key(seed)
    ks = jax.random.split(key, 32)
    n_pages = PAST_LEN // PAGE_SIZE
    n_used = DEC_BATCH * n_pages
    n_pool = n_used + n_used // 4
    f32 = jnp.float32

    def nrm(k, shape, scale=1.0):
        return jax.random.normal(k, shape, f32) * scale

    perm = jax.random.permutation(ks[0], n_pool)
    page_table = perm[:n_used].reshape(DEC_BATCH, n_pages).astype(jnp.int32)
    forget_bias = jnp.linspace(3.0, 6.0, H_A, dtype=f32)[None, :] + nrm(ks[1], (N_EVEN, H_A), 0.01)
    input_bias = nrm(ks[2], (N_EVEN, H_A), 0.1)
    return {
        'x_prompt': nrm(ks[3], (BATCH, SEQ, D_MODEL)),
        'x_sample': nrm(ks[4], (DEC_BATCH, DEC_SEQ, D_MODEL)),
        'state_mlstm_c': nrm(ks[5], (N_EVEN, DEC_BATCH, H_A, DK_A, DV_A), 0.1),
        'state_mlstm_n': nrm(ks[6], (N_EVEN, DEC_BATCH, H_A, DK_A), 0.1),
        'state_mlstm_m': nrm(ks[7], (N_EVEN, DEC_BATCH, H_A)) + 2.0,
        'state_conv': nrm(ks[8], (N_EVEN, DEC_BATCH, CONV_W - 1, D_CONV)),
        'cache_k': nrm(ks[9], (N_ODD, n_pool, PAGE_SIZE, H_C, 2 * DH_C)),
        'cache_v': nrm(ks[10], (N_ODD, n_pool, PAGE_SIZE, H_C, DV_C)),
        'page_table': page_table,
        'w_in_even': nrm(ks[11], (N_EVEN, D_MODEL, IN_EVEN), D_MODEL ** -0.5),
        'b_gate_even': jnp.concatenate([input_bias, forget_bias], axis=-1),
        'conv_w_even': nrm(ks[12], (N_EVEN, CONV_W, D_CONV), CONV_W ** -0.5),
        'w_out_even': nrm(ks[13], (N_EVEN, D_MLSTM + D_CONV, D_MODEL), BETA * (D_MLSTM + D_CONV) ** -0.5),
        'w_qkv_odd': nrm(ks[14], (N_ODD, D_MODEL, 3 * D_ATTN), D_MODEL ** -0.5),
        'w_o_odd': nrm(ks[15], (N_ODD, D_ATTN, D_MODEL), BETA * D_ATTN ** -0.5),
        'lambda_q1': nrm(ks[16], (N_ODD, DH_C), 0.1),
        'lambda_k1': nrm(ks[17], (N_ODD, DH_C), 0.1),
        'lambda_q2': nrm(ks[18], (N_ODD, DH_C), 0.1),
        'lambda_k2': nrm(ks[19], (N_ODD, DH_C), 0.1),
        'subln_g': 1.0 + nrm(ks[20], (N_ODD, DV_C), 0.02),
        'w_ffn_gate': nrm(ks[21], (DEPTH, D_MODEL, D_FF), D_MODEL ** -0.5),
        'w_ffn_up': nrm(ks[22], (DEPTH, D_MODEL, D_FF), D_MODEL ** -0.5),
        'w_ffn_down': nrm(ks[23], (DEPTH, D_FF, D_MODEL), BETA * D_FF ** -0.5),
        'ln_mix_g': 1.0 + nrm(ks[24], (DEPTH, D_MODEL), 0.02),
        'ln_mix_b': nrm(ks[25], (DEPTH, D_MODEL), 0.02),
        'ln_ffn_g': 1.0 + nrm(ks[26], (DEPTH, D_MODEL), 0.02),
        'ln_ffn_b': nrm(ks[27], (DEPTH, D_MODEL), 0.02),
    }


def reference(x_prompt, x_sample, state_mlstm_c, state_mlstm_n, state_mlstm_m, state_conv, cache_k, cache_v, page_table, w_in_even, b_gate_even, conv_w_even, w_out_even, w_qkv_odd, w_o_odd, lambda_q1, lambda_k1, lambda_q2, lambda_k2, subln_g, w_ffn_gate, w_ffn_up, w_ffn_down, ln_mix_g, ln_mix_b, ln_ffn_g, ln_ffn_b):
    slopes = alibi_slopes()
    xp, xs = x_prompt, x_sample
    bp = xp.shape[0]
    pc, pn, pm, pconv, pk, pv = [], [], [], [], [], []
    sc, sn, sm, sconv, sk, sv = [], [], [], [], [], []
    for layer in range(DEPTH):
        j = layer // 2
        if layer % 2 == 0:
            ew = (w_in_even[j], b_gate_even[j], conv_w_even[j], w_out_even[j])
            c0 = jnp.zeros((bp, H_A, DK_A, DV_A), jnp.float32)
            n0 = jnp.zeros((bp, H_A, DK_A), jnp.float32)
            m0 = jnp.zeros((bp, H_A), jnp.float32)
            cv0 = jnp.zeros((bp, CONV_W - 1, D_CONV), xp.dtype)
            mix_p, c1, n1, m1, cv1 = even_mixer(xp, *ew, c0, n0, m0, cv0)
            mix_s, c2, n2, m2, cv2 = even_mixer(xs, *ew, state_mlstm_c[j], state_mlstm_n[j], state_mlstm_m[j], state_conv[j])
            pc.append(c1); pn.append(n1); pm.append(m1); pconv.append(cv1)
            sc.append(c2); sn.append(n2); sm.append(m2); sconv.append(cv2)
        else:
            lam_init = 0.8 - 0.6 * math.exp(-0.3 * layer)
            lam = diff_lambda(lambda_q1[j], lambda_k1[j], lambda_q2[j], lambda_k2[j], lam_init)
            mix_p, k1, v1 = diff_prompt(xp, w_qkv_odd[j], w_o_odd[j], subln_g[j], lam, lam_init, slopes)
            mix_s, k2, v2 = diff_sample(xs, cache_k, cache_v, j, page_table, w_qkv_odd[j], w_o_odd[j], subln_g[j], lam, lam_init, slopes)
            pk.append(k1); pv.append(v1); sk.append(k2); sv.append(v2)
        xp = post_norm(xp, mix_p, ln_mix_g[layer], ln_mix_b[layer])
        xs = post_norm(xs, mix_s, ln_mix_g[layer], ln_mix_b[layer])
        xp = post_norm(xp, swiglu(xp, w_ffn_gate[layer], w_ffn_up[layer], w_ffn_down[layer]), ln_ffn_g[layer], ln_ffn_b[layer])
        xs = post_norm(xs, swiglu(xs, w_ffn_gate[layer], w_ffn_up[layer], w_ffn_down[layer]), ln_ffn_g[layer], ln_ffn_b[layer])
    return (xp, xs, jnp.stack(pc), jnp.stack(pn), jnp.stack(pm), jnp.stack(pconv), jnp.stack(pk), jnp.stack(pv), jnp.stack(sc), jnp.stack(sn), jnp.stack(sm), jnp.stack(sconv), jnp.stack(sk), jnp.stack(sv))
```

```python
import functools
import math

import jax
import jax.numpy as jnp
from jax import lax
from jax.experimental import pallas as pl
from jax.experimental.pallas import tpu as pltpu

F32 = jnp.float32
BF16 = jnp.bfloat16

LN_EPS = 1e-5
CONV_W = 3
V7X_VMEM_LIMIT_BYTES = 56 * 1024 * 1024
LANES = 128
SUBLANES = 8


def _params(*sem):
    return pltpu.CompilerParams(dimension_semantics=sem, vmem_limit_bytes=V7X_VMEM_LIMIT_BYTES)


def _layer_norm(y, g, b):
    mu = jnp.mean(y, axis=-1, keepdims=True)
    d = y - mu
    var = jnp.mean(d * d, axis=-1, keepdims=True)
    return d * lax.rsqrt(var + LN_EPS) * g + b


def _log_sigmoid(x):
    return jnp.minimum(x, 0.0) - jnp.log1p(jnp.exp(-jnp.abs(x)))


def _tile(n, pref):
    if n <= pref:
        return n
    t = pref
    while n % t:
        t //= 2
    return t


def _mm_kernel(x_ref, w_ref, *refs, scale, cast_x, n_out):
    out_refs = refs[:n_out]
    if cast_x:
        xb_ref = refs[n_out]

        @pl.when(pl.program_id(1) == 0)
        def _():
            xb_ref[...] = x_ref[...].astype(BF16)

        x = xb_ref[...]
    else:
        x = x_ref[...]
    acc = jnp.dot(x, w_ref[...], preferred_element_type=F32)
    if scale is not None:
        acc = acc * scale
    for o_ref in out_refs:
        o_ref[...] = acc.astype(o_ref.dtype)


def _mm(x, w, out_dtypes, scale=None, tm_pref=1024, tn_pref=512):
    m, k = x.shape
    n = w.shape[1]
    tm, tn = _tile(m, tm_pref), _tile(n, tn_pref)
    cast_x = x.dtype != BF16
    kern = functools.partial(_mm_kernel, scale=scale, cast_x=cast_x, n_out=len(out_dtypes))
    outs = pl.pallas_call(
        kern,
        grid=(m // tm, n // tn),
        in_specs=[pl.BlockSpec((tm, k), lambda i, j: (i, 0)),
                  pl.BlockSpec((k, tn), lambda i, j: (0, j))],
        out_specs=[pl.BlockSpec((tm, tn), lambda i, j: (i, j)) for _ in out_dtypes],
        out_shape=[jax.ShapeDtypeStruct((m, n), dt) for dt in out_dtypes],
        scratch_shapes=[pltpu.VMEM((tm, k), BF16)] if cast_x else [],
        compiler_params=_params("parallel", "arbitrary"),
        name="mm",
    )(x, w)
    return outs


def _mm_ln_kernel(x_ref, w_ref, res_ref, g_ref, b_ref, of_ref, ob_ref, *, alpha):
    f = jnp.dot(x_ref[...], w_ref[...], preferred_element_type=F32)
    y = _layer_norm(alpha * res_ref[...] + f, g_ref[...], b_ref[...])
    of_ref[...] = y
    ob_ref[...] = y.astype(BF16)


def _mm_ln(x, w, res, g, b, alpha, tm_pref=256):
    m, k = x.shape
    d = w.shape[1]
    tm = _tile(m, tm_pref)
    return pl.pallas_call(
        functools.partial(_mm_ln_kernel, alpha=alpha),
        grid=(m // tm,),
        in_specs=[pl.BlockSpec((tm, k), lambda i: (i, 0)),
                  pl.BlockSpec((k, d), lambda i: (0, 0)),
                  pl.BlockSpec((tm, d), lambda i: (i, 0)),
                  pl.BlockSpec((1, d), lambda i: (0, 0)),
                  pl.BlockSpec((1, d), lambda i: (0, 0))],
        out_specs=[pl.BlockSpec((tm, d), lambda i: (i, 0)),
                   pl.BlockSpec((tm, d), lambda i: (i, 0))],
        out_shape=[jax.ShapeDtypeStruct((m, d), F32), jax.ShapeDtypeStruct((m, d), BF16)],
        compiler_params=_params("parallel"),
        name="mm_ln",
    )(x, w, res, g, b)


def _ffn_kernel(x_ref, wg_ref, wu_ref, wd_ref, res_ref, g_ref, b_ref, of_ref, ob_ref, acc_ref, *, alpha):
    f = pl.program_id(1)

    @pl.when(f == 0)
    def _():
        acc_ref[...] = jnp.zeros_like(acc_ref)

    x = x_ref[...]
    gate = jnp.dot(x, wg_ref[...], preferred_element_type=F32)
    up = jnp.dot(x, wu_ref[...], preferred_element_type=F32)
    h = (gate * jax.nn.sigmoid(gate) * up).astype(BF16)
    acc_ref[...] += jnp.dot(h, wd_ref[...], preferred_element_type=F32)

    @pl.when(f == pl.num_programs(1) - 1)
    def _():
        y = _layer_norm(alpha * res_ref[...] + acc_ref[...], g_ref[...], b_ref[...])
        of_ref[...] = y
        ob_ref[...] = y.astype(BF16)


def _ffn(x, wg, wu, wd, res, g, b, alpha, tm_pref=512, tf_pref=512):
    m, d = x.shape
    ff = wg.shape[1]
    tm, tf = _tile(m, tm_pref), _tile(ff, tf_pref)
    return pl.pallas_call(
        functools.partial(_ffn_kernel, alpha=alpha),
        grid=(m // tm, ff // tf),
        in_specs=[pl.BlockSpec((tm, d), lambda i, f: (i, 0)),
                  pl.BlockSpec((d, tf), lambda i, f: (0, f)),
                  pl.BlockSpec((d, tf), lambda i, f: (0, f)),
                  pl.BlockSpec((tf, d), lambda i, f: (f, 0)),
                  pl.BlockSpec((tm, d), lambda i, f: (i, 0)),
                  pl.BlockSpec((1, d), lambda i, f: (0, 0)),
                  pl.BlockSpec((1, d), lambda i, f: (0, 0))],
        out_specs=[pl.BlockSpec((tm, d), lambda i, f: (i, 0)),
                   pl.BlockSpec((tm, d), lambda i, f: (i, 0))],
        out_shape=[jax.ShapeDtypeStruct((m, d), F32), jax.ShapeDtypeStruct((m, d), BF16)],
        scratch_shapes=[pltpu.VMEM((tm, d), F32)],
        compiler_params=_params("parallel", "arbitrary"),
        name="ffn",
    )(x, wg, wu, wd, res, g, b)


def _evenmix_kernel(qkv_ref, rest_ref, gate_ref, bg_ref, cw_ref,
                    mix_ref, c_out, n_out, m_out, cv_out,
                    c_s, n_s, m_s, u_s, *, nh, dk, dv, dc, chunk):
    c = pl.program_id(1)
    last = pl.num_programs(1) - 1
    dm = nh * dk
    L = chunk

    @pl.when(c == 0)
    def _():
        c_s[...] = jnp.zeros_like(c_s)
        n_s[...] = jnp.zeros_like(n_s)
        m_s[...] = jnp.zeros_like(m_s)
        u_s[...] = jnp.zeros_like(u_s)

    g = gate_ref[...] + bg_ref[...]
    lane = lax.broadcasted_iota(jnp.int32, (L, LANES), 1)
    row = lax.broadcasted_iota(jnp.int32, (L, LANES), 0)
    bcum = _log_sigmoid(g)
    s = 1
    while s < L:
        bcum = bcum + jnp.where(row >= s, pltpu.roll(bcum, s, 0), 0.0)
        s *= 2
    z = jnp.where(lane < nh, g, bcum)
    zt = z.T

    tr = lax.broadcasted_iota(jnp.int32, (L, L), 0)
    tc = lax.broadcasted_iota(jnp.int32, (L, L), 1)
    causal = tr >= tc
    lane1 = lax.broadcasted_iota(jnp.int32, (1, LANES), 1)
    m_vec = jnp.zeros((1, LANES), F32)

    for h in range(nh):
        qc = qkv_ref[:, h * dk:(h + 1) * dk]
        kc = qkv_ref[:, dm + h * dk:dm + (h + 1) * dk] * (dk ** -0.5)
        vc = qkv_ref[:, 2 * dm + h * dv:2 * dm + (h + 1) * dv]
        ig_col = z[:, h:h + 1]
        b_col = z[:, nh + h:nh + h + 1]
        ig_row = zt[h:h + 1, :]
        b_row = zt[nh + h:nh + h + 1, :]
        m_prev = m_s[h][:, 0:1]
        cmat = c_s[h]
        n_row = n_s[h]

        dmat = jnp.where(causal, b_col - b_row + ig_row, -jnp.inf)
        inter = b_col + m_prev
        m_t = jnp.maximum(inter, jnp.max(dmat, axis=-1, keepdims=True))
        w_intra = jnp.exp(dmat - m_t)
        w_inter = jnp.exp(inter - m_t)
        sc = lax.dot_general(qc, kc, (((1,), (1,)), ((), ())), preferred_element_type=F32) * w_intra
        num = (w_inter * jnp.dot(qc, cmat.astype(BF16), preferred_element_type=F32)
               + jnp.dot(sc.astype(BF16), vc, preferred_element_type=F32))
        qn = jnp.sum(qc.astype(F32) * n_row, axis=-1, keepdims=True)
        den = w_inter * qn + jnp.sum(sc, axis=-1, keepdims=True)
        out = num / jnp.maximum(jnp.abs(den), jnp.exp(-m_t))

        m_new = m_t[L - 1:L, :]
        b_last = b_col[L - 1:L, :]
        g_inter = jnp.exp(b_last + m_prev - m_new)
        g_s = jnp.exp(b_last - b_col + ig_col - m_new)
        kg = kc.astype(F32) * g_s
        c_s[h] = g_inter * cmat + jnp.dot(kg.T.astype(BF16), vc, preferred_element_type=F32)
        n_s[h] = g_inter * n_row + jnp.sum(kg, axis=0, keepdims=True)
        m_s[h] = jnp.broadcast_to(m_new, (1, LANES))
        m_vec = jnp.where(lane1 == h, m_new, m_vec)

        o_gate = rest_ref[:, h * dv:(h + 1) * dv]
        mix_ref[:, h * dv:(h + 1) * dv] = (jax.nn.sigmoid(o_gate) * out).astype(BF16)

    base = nh * dv
    gb = rest_ref[:, base:base + dc]
    u = rest_ref[:, base + dc:base + 2 * dc] * rest_ref[:, base + 2 * dc:base + 3 * dc]
    prev2 = u_s[0:1, :]
    prev1 = u_s[1:2, :]
    rowc = lax.broadcasted_iota(jnp.int32, (L, dc), 0)
    u1 = jnp.where(rowc == 0, prev1, pltpu.roll(u, 1, 0))
    u2 = jnp.where(rowc == 0, prev2, jnp.where(rowc == 1, prev1, pltpu.roll(u, 2, 0)))
    yc = gb * (cw_ref[0:1, :] * u2 + cw_ref[1:2, :] * u1 + cw_ref[2:3, :] * u)
    mix_ref[:, dm:dm + dc] = yc.astype(BF16)
    u_s[0:1, :] = u[L - 2:L - 1, :]
    u_s[1:2, :] = u[L - 1:L, :]

    @pl.when(c == last)
    def _():
        c_out[0] = c_s[...]
        n_out[0] = n_s[...]
        m_out[0] = m_vec
        cv_out[0, 0:1, :] = u[L - 2:L - 1, :]
        cv_out[0, 1:2, :] = u[L - 1:L, :]


def _evenmix_prompt(qkv, rest, gates, b_gate_pad, conv_w, bsz, t, nh, dk, dv, dc, chunk):
    nc = t // chunk
    dm = nh * dk
    kern = functools.partial(_evenmix_kernel, nh=nh, dk=dk, dv=dv, dc=dc, chunk=chunk)
    return pl.pallas_call(
        kern,
        grid=(bsz, nc),
        in_specs=[pl.BlockSpec((chunk, 3 * dm), lambda b, c: (b * nc + c, 0)),
                  pl.BlockSpec((chunk, dm + 3 * dc), lambda b, c: (b * nc + c, 0)),
                  pl.BlockSpec((chunk, LANES), lambda b, c: (b * nc + c, 0)),
                  pl.BlockSpec((1, LANES), lambda b, c: (0, 0)),
                  pl.BlockSpec((CONV_W, dc), lambda b, c: (0, 0))],
        out_specs=[pl.BlockSpec((chunk, dm + dc), lambda b, c: (b * nc + c, 0)),
                   pl.BlockSpec((1, nh, dk, dv), lambda b, c: (b, 0, 0, 0)),
                   pl.BlockSpec((1, nh, 1, dk), lambda b, c: (b, 0, 0, 0)),
                   pl.BlockSpec((1, 1, LANES), lambda b, c: (b, 0, 0)),
                   pl.BlockSpec((1, CONV_W - 1, dc), lambda b, c: (b, 0, 0))],
        out_shape=[jax.ShapeDtypeStruct((bsz * t, dm + dc), BF16),
                   jax.ShapeDtypeStruct((bsz, nh, dk, dv), F32),
                   jax.ShapeDtypeStruct((bsz, nh, 1, dk), F32),
                   jax.ShapeDtypeStruct((bsz, 1, LANES), F32),
                   jax.ShapeDtypeStruct((bsz, CONV_W - 1, dc), F32)],
        scratch_shapes=[pltpu.VMEM((nh, dk, dv), F32),
                        pltpu.VMEM((nh, 1, dk), F32),
                        pltpu.VMEM((nh, 1, LANES), F32),
                        pltpu.VMEM((SUBLANES, dc), F32)],
        compiler_params=_params("parallel", "arbitrary"),
        name="evenmix_prompt",
    )(qkv, rest, gates, b_gate_pad, conv_w)


def _evenmix_step_kernel(qcol_ref, kcol_ref, qrow_ref, krow_ref, vrow_ref, orow_ref, conv_ref,
                         gate_ref, bg_ref, cw_ref, c_ref, n_ref, m_ref, cv_ref,
                         mix_ref, c_out, n_out, m_out, cv_out, *, nh, dk, dv, dc):
    dm = nh * dk
    g = gate_ref[0] + bg_ref[...]
    lf_all = _log_sigmoid(g)
    lane1 = lax.broadcasted_iota(jnp.int32, (1, LANES), 1)
    m_vec = jnp.zeros((1, LANES), F32)
    scale = dk ** -0.5
    for h in range(nh):
        q_col = qcol_ref[0, h]
        k_col = kcol_ref[0, h] * scale
        q_row = qrow_ref[0, h]
        k_row = krow_ref[0, h] * scale
        v_row = vrow_ref[0, h]
        ig = g[:, h:h + 1]
        lf = lf_all[:, nh + h:nh + h + 1]
        m_prev = m_ref[0][:, h:h + 1]
        cmat = c_ref[0, h]
        n_row = n_ref[0, h]

        inter = lf + m_prev
        m_t = jnp.maximum(inter, ig)
        w_intra = jnp.exp(ig - m_t)
        w_inter = jnp.exp(inter - m_t)
        sc = jnp.sum(q_row * k_row, axis=-1, keepdims=True) * w_intra
        num = w_inter * jnp.sum(q_col * cmat, axis=0, keepdims=True) + sc * v_row
        den = w_inter * jnp.sum(q_row * n_row, axis=-1, keepdims=True) + sc
        out = num / jnp.maximum(jnp.abs(den), jnp.exp(-m_t))

        g_inter = jnp.exp(lf + m_prev - m_t)
        g_s = jnp.exp(ig - m_t)
        c_out[0, h] = g_inter * cmat + (g_s * k_col) * v_row
        n_out[0, h] = g_inter * n_row + g_s * k_row
        m_vec = jnp.where(lane1 == h, m_t, m_vec)
        mix_ref[0, :, h * dv:(h + 1) * dv] = jax.nn.sigmoid(orow_ref[0, h]) * out
    m_out[0] = m_vec

    gb = conv_ref[0, 0:1, :]
    u = conv_ref[0, 1:2, :] * conv_ref[0, 2:3, :]
    prev2 = cv_ref[0, 0:1, :]
    prev1 = cv_ref[0, 1:2, :]
    yc = gb * (cw_ref[0:1, :] * prev2 + cw_ref[1:2, :] * prev1 + cw_ref[2:3, :] * u)
    mix_ref[0, :, dm:dm + dc] = yc
    cv_out[0, 0:1, :] = prev1
    cv_out[0, 1:2, :] = u


def _evenmix_step(q, k, v, o, conv_in, gates, b_gate_pad, conv_w, c0, n0, m0_pad, cv0, nh, dk, dv, dc):
    bsz = q.shape[0]
    dm = nh * dk
    qcol = q.reshape(bsz, nh, dk, 1)
    kcol = k.reshape(bsz, nh, dk, 1)
    qrow = q.reshape(bsz, nh, 1, dk)
    krow = k.reshape(bsz, nh, 1, dk)
    vrow = v.reshape(bsz, nh, 1, dv)
    orow = o.reshape(bsz, nh, 1, dv)
    col_spec = pl.BlockSpec((1, nh, dk, 1), lambda b: (b, 0, 0, 0))
    row_spec = pl.BlockSpec((1, nh, 1, dk), lambda b: (b, 0, 0, 0))
    vrow_spec = pl.BlockSpec((1, nh, 1, dv), lambda b: (b, 0, 0, 0))
    kern = functools.partial(_evenmix_step_kernel, nh=nh, dk=dk, dv=dv, dc=dc)
    return pl.pallas_call(
        kern,
        grid=(bsz,),
        in_specs=[col_spec, col_spec, row_spec, row_spec, vrow_spec, vrow_spec,
                  pl.BlockSpec((1, 3, dc), lambda b: (b, 0, 0)),
                  pl.BlockSpec((1, 1, LANES), lambda b: (b, 0, 0)),
                  pl.BlockSpec((1, LANES), lambda b: (0, 0)),
                  pl.BlockSpec((CONV_W, dc), lambda b: (0, 0)),
                  pl.BlockSpec((1, nh, dk, dv), lambda b: (b, 0, 0, 0)),
                  pl.BlockSpec((1, nh, 1, dk), lambda b: (b, 0, 0, 0)),
                  pl.BlockSpec((1, 1, LANES), lambda b: (b, 0, 0)),
                  pl.BlockSpec((1, CONV_W - 1, dc), lambda b: (b, 0, 0))],
        out_specs=[pl.BlockSpec((1, 1, dm + dc), lambda b: (b, 0, 0)),
                   pl.BlockSpec((1, nh, dk, dv), lambda b: (b, 0, 0, 0)),
                   pl.BlockSpec((1, nh, 1, dk), lambda b: (b, 0, 0, 0)),
                   pl.BlockSpec((1, 1, LANES), lambda b: (b, 0, 0)),
                   pl.BlockSpec((1, CONV_W - 1, dc), lambda b: (b, 0, 0))],
        out_shape=[jax.ShapeDtypeStruct((bsz, 1, dm + dc), F32),
                   jax.ShapeDtypeStruct((bsz, nh, dk, dv), F32),
                   jax.ShapeDtypeStruct((bsz, nh, 1, dk), F32),
                   jax.ShapeDtypeStruct((bsz, 1, LANES), F32),
                   jax.ShapeDtypeStruct((bsz, CONV_W - 1, dc), F32)],
        compiler_params=_params("parallel"),
        name="evenmix_step",
    )(qcol, kcol, qrow, krow, vrow, orow, conv_in, gates, b_gate_pad, conv_w, c0, n0, m0_pad, cv0)


def _diff_lambda(lamv_ref, lam_init):
    a = jnp.sum(lamv_ref[0:1, :] * lamv_ref[1:2, :], axis=-1, keepdims=True)
    b = jnp.sum(lamv_ref[2:3, :] * lamv_ref[3:4, :], axis=-1, keepdims=True)
    return jnp.exp(a) - jnp.exp(b) + lam_init


def _attn_prompt_kernel(slope_ref, q_ref, k_ref, v_ref, lamv_ref, sg_ref, o_ref,
                        m_s, l_s, acc_s, *, dh, tq, tk, lam_init):
    h = pl.program_id(1)
    qi = pl.program_id(2)
    ki = pl.program_id(3)

    @pl.when(ki == 0)
    def _():
        m_s[...] = jnp.full_like(m_s, -jnp.inf)
        l_s[...] = jnp.zeros_like(l_s)
        acc_s[...] = jnp.zeros_like(acc_s)

    @pl.when(ki * tk <= qi * tq + (tq - 1))
    def _():
        slope = slope_ref[h]
        dist = ((qi * tq - ki * tk)
                + lax.broadcasted_iota(jnp.int32, (tq, tk), 0)
                - lax.broadcasted_iota(jnp.int32, (tq, tk), 1))
        bias = jnp.where(dist >= 0, -slope * dist.astype(F32), -jnp.inf)
        v = v_ref[...]
        for c in range(2):
            s = lax.dot_general(q_ref[:, c * dh:(c + 1) * dh], k_ref[:, c * dh:(c + 1) * dh],
                                (((1,), (1,)), ((), ())), preferred_element_type=F32) + bias
            m_prev = m_s[c]
            m_new = jnp.maximum(m_prev, jnp.max(s, axis=-1, keepdims=True))
            alpha = jnp.exp(m_prev - m_new)
            p = jnp.exp(s - m_new)
            l_s[c] = alpha * l_s[c] + jnp.sum(p, axis=-1, keepdims=True)
            acc_s[c] = alpha * acc_s[c] + jnp.dot(p.astype(BF16), v, preferred_element_type=F32)
            m_s[c] = m_new

    @pl.when(ki == pl.num_programs(3) - 1)
    def _():
        lam = _diff_lambda(lamv_ref, lam_init)
        o = acc_s[0] / l_s[0] - lam * (acc_s[1] / l_s[1])
        o = o * lax.rsqrt(jnp.mean(o * o, axis=-1, keepdims=True) + LN_EPS) * sg_ref[...]
        o_ref[...] = (o * (1.0 - lam_init)).astype(BF16)


def _attn_prompt(q, k, v, slopes, lamv, subln_g, bsz, t, nh, dh, dv, lam_init, tq_pref=256):
    tq = _tile(t, tq_pref)
    tk = tq
    nq, nk = t // tq, t // tk
    kern = functools.partial(_attn_prompt_kernel, dh=dh, tq=tq, tk=tk, lam_init=lam_init)

    def kv_map(b, h, qi, ki):
        return (b * nk + jnp.minimum(ki, (qi * tq + tq - 1) // tk), h)

    return pl.pallas_call(
        kern,
        grid=(bsz, nh, nq, nk),
        in_specs=[pl.BlockSpec(memory_space=pltpu.SMEM),
                  pl.BlockSpec((tq, 2 * dh), lambda b, h, qi, ki: (b * nq + qi, h)),
                  pl.BlockSpec((tk, 2 * dh), kv_map),
                  pl.BlockSpec((tk, dv), kv_map),
                  pl.BlockSpec((4, dh), lambda b, h, qi, ki: (0, 0)),
                  pl.BlockSpec((1, dv), lambda b, h, qi, ki: (0, 0))],
        out_specs=pl.BlockSpec((tq, dv), lambda b, h, qi, ki: (b * nq + qi, h)),
        out_shape=jax.ShapeDtypeStruct((bsz * t, nh * dv), BF16),
        scratch_shapes=[pltpu.VMEM((2, tq, 1), F32),
                        pltpu.VMEM((2, tq, 1), F32),
                        pltpu.VMEM((2, tq, dv), F32)],
        compiler_params=_params("parallel", "parallel", "parallel", "arbitrary"),
        name="attn_prompt",
    )(slopes, q, k, v, lamv, subln_g)


def _attn_step_kernel(pt_ref, q_ref, kn_ref, vn_ref, slope_ref, lamv_ref, sg_ref, *refs,
                      dh, dv, page, pages_per_step, past, lam_init):
    del pt_ref
    k_refs = refs[:pages_per_step]
    v_refs = refs[pages_per_step:2 * pages_per_step]
    o_ref = refs[2 * pages_per_step]
    m_s, l_s, acc_s, pos_s = refs[2 * pages_per_step + 1:]
    pp = pl.program_id(1)
    nh = q_ref.shape[1]
    nhalf = dv // LANES
    ones = jnp.ones((dh, LANES), F32)
    slope = slope_ref[...]

    def lane_sum(x2d):
        return jnp.dot(x2d, ones, preferred_element_type=F32)

    @pl.when(pp == 0)
    def _():
        pos_s[...] = lax.broadcasted_iota(jnp.int32, (page, nh, LANES), 0).astype(F32) * slope[None]
        for c in range(2):
            s_self = lane_sum(q_ref[0, :, c * dh:(c + 1) * dh] * kn_ref[0, :, c * dh:(c + 1) * dh])
            m_s[c] = s_self
            l_s[c] = jnp.ones((nh, LANES), F32)
            for e in range(nhalf):
                acc_s[c, e] = vn_ref[0, :, e * LANES:(e + 1) * LANES]

    for i in range(pages_per_step):
        k_ref, v_ref = k_refs[i], v_refs[i]
        first_pos = (pp * pages_per_step + i) * page - past
        page_bias = slope * first_pos.astype(F32)
        for c in range(2):
            prod = k_ref[:, :, c * dh:(c + 1) * dh] * q_ref[0, :, c * dh:(c + 1) * dh][None]
            s = lane_sum(prod.reshape(page * nh, dh)).reshape(page, nh, LANES) + pos_s[...]
            m_prev = m_s[c]
            m_new = jnp.maximum(m_prev, jnp.max(s, axis=0) + page_bias)
            alpha = jnp.exp(m_prev - m_new)
            p = jnp.exp(s - (m_new - page_bias)[None])
            l_s[c] = alpha * l_s[c] + jnp.sum(p, axis=0)
            for e in range(nhalf):
                acc_s[c, e] = alpha * acc_s[c, e] + jnp.sum(p * v_ref[:, :, e * LANES:(e + 1) * LANES], axis=0)
            m_s[c] = m_new

    @pl.when(pp == pl.num_programs(1) - 1)
    def _():
        lam = _diff_lambda(lamv_ref, lam_init)
        halves = [acc_s[0, e] / l_s[0] - lam * (acc_s[1, e] / l_s[1]) for e in range(nhalf)]
        ssq = sum(jnp.sum(o * o, axis=-1, keepdims=True) for o in halves)
        inv = lax.rsqrt(ssq / dv + LN_EPS) * (1.0 - lam_init)
        for e in range(nhalf):
            o_ref[0, :, e * LANES:(e + 1) * LANES] = halves[e] * inv * sg_ref[:, e * LANES:(e + 1) * LANES]


def _attn_step(q, kn, vn, cache_k, cache_v, layer_j, page_table, slopes_rep, lamv, subln_g,
               nh, dh, dv, lam_init, pages_per_step=2):
    bsz, n_pages = page_table.shape
    page = cache_k.shape[2]
    past = n_pages * page
    pps = pages_per_step if n_pages % pages_per_step == 0 else 1
    kern = functools.partial(_attn_step_kernel, dh=dh, dv=dv, page=page, pages_per_step=pps,
                             past=past, lam_init=lam_init)

    def page_spec(i, width):
        return pl.BlockSpec((None, None, page, nh, width),
                            lambda b, pp, pt: (layer_j, pt[b * n_pages + pp * pps + i], 0, 0, 0))

    tok_spec = lambda width: pl.BlockSpec((1, nh, width), lambda b, pp, pt: (b, 0, 0))
    grid_spec = pltpu.PrefetchScalarGridSpec(
        num_scalar_prefetch=1,
        grid=(bsz, n_pages // pps),
        in_specs=[tok_spec(2 * dh), tok_spec(2 * dh), tok_spec(dv),
                  pl.BlockSpec((nh, LANES), lambda b, pp, pt: (0, 0)),
                  pl.BlockSpec((4, dh), lambda b, pp, pt: (0, 0)),
                  pl.BlockSpec((1, dv), lambda b, pp, pt: (0, 0))]
                 + [page_spec(i, 2 * dh) for i in range(pps)]
                 + [page_spec(i, dv) for i in range(pps)],
        out_specs=tok_spec(dv),
        scratch_shapes=[pltpu.VMEM((2, nh, LANES), F32),
                        pltpu.VMEM((2, nh, LANES), F32),
                        pltpu.VMEM((2, dv // LANES, nh, LANES), F32),
                        pltpu.VMEM((page, nh, LANES), F32)],
    )
    return pl.pallas_call(
        kern,
        grid_spec=grid_spec,
        out_shape=jax.ShapeDtypeStruct((bsz, nh, dv), F32),
        compiler_params=_params("parallel", "arbitrary"),
        name="attn_step",
    )(page_table.reshape(-1), q, kn, vn, slopes_rep, lamv, subln_g,
      *([cache_k] * pps), *([cache_v] * pps))


def kernel(x_prompt, x_sample, state_mlstm_c, state_mlstm_n, state_mlstm_m, state_conv, cache_k, cache_v, page_table, w_in_even, b_gate_even, conv_w_even, w_out_even, w_qkv_odd, w_o_odd, lambda_q1, lambda_k1, lambda_q2, lambda_k2, subln_g, w_ffn_gate, w_ffn_up, w_ffn_down, ln_mix_g, ln_mix_b, ln_ffn_g, ln_ffn_b):
    bp, t, d = x_prompt.shape
    bs = x_sample.shape[0]
    assert x_sample.shape[1] == 1, "the sample group decodes one token per sequence"
    depth = w_ffn_gate.shape[0]
    nh_a, dk_a, dv_a = state_mlstm_c.shape[2:]
    dm = nh_a * dk_a
    dc = state_conv.shape[-1]
    nh_c, dv_c = cache_v.shape[3:]
    dh_c = cache_k.shape[4] // 2
    alpha = (2 * depth) ** 0.25
    chunk = math.gcd(t, 128)

    xp = x_prompt.reshape(bp * t, d)
    xs = x_sample.reshape(bs, d)
    xp_b = xs_b = None
    slopes = jnp.exp2(-8.0 * jnp.arange(1, nh_c + 1, dtype=F32) / nh_c)
    slopes_rep = jnp.broadcast_to(slopes[:, None], (nh_c, LANES))

    pc, pn, pm, pconv, pk, pv = [], [], [], [], [], []
    sc, sn, sm, sconv, sk, sv = [], [], [], [], [], []
    for layer in range(depth):
        j = layer // 2
        g_mix, b_mix = ln_mix_g[layer][None], ln_mix_b[layer][None]
        g_ffn, b_ffn = ln_ffn_g[layer][None], ln_ffn_b[layer][None]
        if layer % 2 == 0:
            w_in = w_in_even[j]
            w_qkv = w_in[:, :3 * dm].astype(BF16)
            gate_lo = 4 * dm
            w_gate = jnp.pad(w_in[:, gate_lo:gate_lo + 2 * nh_a], ((0, 0), (0, LANES - 2 * nh_a))).astype(BF16)
            w_rest = jnp.concatenate([w_in[:, 3 * dm:4 * dm], w_in[:, gate_lo + 2 * nh_a:]], axis=1).astype(BF16)
            w_out = w_out_even[j].astype(BF16)
            b_gate_pad = jnp.pad(b_gate_even[j], (0, LANES - 2 * nh_a))[None]
            conv_w = conv_w_even[j]

            x_in = xp if xp_b is None else xp_b
            (qkv,) = _mm(x_in, w_qkv, [BF16])
            (rest,) = _mm(x_in, w_rest, [F32])
            (gates,) = _mm(x_in, w_gate, [F32])
            mix, c1, n1, m1, cv1 = _evenmix_prompt(qkv, rest, gates, b_gate_pad, conv_w,
                                                   bp, t, nh_a, dk_a, dv_a, dc, chunk)
            xp, xp_b = _mm_ln(mix, w_out, xp, g_mix, b_mix, alpha)
            pc.append(c1); pn.append(n1.reshape(bp, nh_a, dk_a)); pm.append(m1[:, 0, :nh_a]); pconv.append(cv1)

            x_in = xs if xs_b is None else xs_b
            (qkv,) = _mm(x_in, w_qkv, [F32])
            (rest,) = _mm(x_in, w_rest, [F32])
            (gates,) = _mm(x_in, w_gate, [F32])
            conv_in = rest[:, dm:].reshape(bs, 3, dc)
            m0_pad = jnp.pad(state_mlstm_m[j], ((0, 0), (0, LANES - nh_a)))[:, None, :]
            mix, c2, n2, m2, cv2 = _evenmix_step(
                qkv[:, :dm], qkv[:, dm:2 * dm], qkv[:, 2 * dm:], rest[:, :dm], conv_in,
                gates[:, None, :], b_gate_pad, conv_w, state_mlstm_c[j],
                state_mlstm_n[j][:, :, None, :], m0_pad, state_conv[j], nh_a, dk_a, dv_a, dc)
            xs, xs_b = _mm_ln(mix.reshape(bs, dm + dc).astype(BF16), w_out, xs, g_mix, b_mix, alpha)
            sc.append(c2); sn.append(n2.reshape(bs, nh_a, dk_a)); sm.append(m2[:, 0, :nh_a]); sconv.append(cv2)
        else:
            lam_init = 0.8 - 0.6 * math.exp(-0.3 * layer)
            d_attn = nh_c * dv_c
            w_qkv = w_qkv_odd[j]
            w_q = w_qkv[:, :d_attn].astype(BF16)
            w_k = w_qkv[:, d_attn:2 * d_attn].astype(BF16)
            w_v = w_qkv[:, 2 * d_attn:].astype(BF16)
            w_o = w_o_odd[j].astype(BF16)
            lamv = jnp.stack([lambda_q1[j], lambda_k1[j], lambda_q2[j], lambda_k2[j]])
            sg = subln_g[j][None]
            q_scale = dh_c ** -0.5

            x_in = xp if xp_b is None else xp_b
            (q_b,) = _mm(x_in, w_q, [BF16], scale=q_scale)
            k_f, k_b = _mm(x_in, w_k, [F32, BF16])
            v_f, v_b = _mm(x_in, w_v, [F32, BF16])
            att = _attn_prompt(q_b, k_b, v_b, slopes, lamv, sg, bp, t, nh_c, dh_c, dv_c, lam_init)
            xp, xp_b = _mm_ln(att, w_o, xp, g_mix, b_mix, alpha)
            pk.append(k_f.reshape(bp, t, nh_c, 2 * dh_c)); pv.append(v_f.reshape(bp, t, nh_c, dv_c))

            x_in = xs if xs_b is None else xs_b
            (q_s,) = _mm(x_in, w_q, [F32], scale=q_scale)
            (k_s,) = _mm(x_in, w_k, [F32])
            (v_s,) = _mm(x_in, w_v, [F32])
            att = _attn_step(q_s.reshape(bs, nh_c, 2 * dh_c), k_s.reshape(bs, nh_c, 2 * dh_c),
                             v_s.reshape(bs, nh_c, dv_c), cache_k, cache_v, j, page_table,
                             slopes_rep, lamv, sg, nh_c, dh_c, dv_c, lam_init)
            xs, xs_b = _mm_ln(att.reshape(bs, d_attn).astype(BF16), w_o, xs, g_mix, b_mix, alpha)
            sk.append(k_s.reshape(bs, 1, nh_c, 2 * dh_c)); sv.append(v_s.reshape(bs, 1, nh_c, dv_c))

        wg = w_ffn_gate[layer].astype(BF16)
        wu = w_ffn_up[layer].astype(BF16)
        wd = w_ffn_down[layer].astype(BF16)
        xp, xp_b = _ffn(xp_b, wg, wu, wd, xp, g_ffn, b_ffn, alpha)
        xs, xs_b = _ffn(xs_b, wg, wu, wd, xs, g_ffn, b_ffn, alpha)

    return (xp.reshape(bp, t, d), xs.reshape(bs, 1, d),
            jnp.stack(pc), jnp.stack(pn), jnp.stack(pm), jnp.stack(pconv), jnp.stack(pk), jnp.stack(pv),
            jnp.stack(sc), jnp.stack(sn), jnp.stack(sm), jnp.stack(sconv), jnp.stack(sk), jnp.stack(sv))
```

```python
import functools
import math

import jax
import jax.numpy as jnp
from jax import lax
from jax.experimental import pallas as pl
from jax.experimental.pallas import tpu as pltpu

F32 = jnp.float32
BF16 = jnp.bfloat16

LN_EPS = 1e-5
CONV_W = 3
LOG2E = math.log2(math.e)
V7X_VMEM_LIMIT_BYTES = 56 * 1024 * 1024
LANES = 128
SUBLANES = 8


def _params(*sem):
    return pltpu.CompilerParams(dimension_semantics=sem, vmem_limit_bytes=V7X_VMEM_LIMIT_BYTES)


def _layer_norm(y, g, b):
    mu = jnp.mean(y, axis=-1, keepdims=True)
    d = y - mu
    var = jnp.mean(d * d, axis=-1, keepdims=True)
    return d * lax.rsqrt(var + LN_EPS) * g + b


def _log_sigmoid(x):
    return jnp.minimum(x, 0.0) - jnp.log1p(jnp.exp(-jnp.abs(x)))


def _tile(n, pref):
    if n <= pref:
        return n
    t = pref
    while n % t:
        t //= 2
    return t


def _mm_kernel(x_ref, w_ref, *refs, scale, cast_x, n_out):
    out_refs = refs[:n_out]
    if cast_x:
        xb_ref = refs[n_out]

        @pl.when(pl.program_id(1) == 0)
        def _():
            xb_ref[...] = x_ref[...].astype(BF16)

        x = xb_ref[...]
    else:
        x = x_ref[...]
    acc = jnp.dot(x, w_ref[...], preferred_element_type=F32)
    if scale is not None:
        acc = acc * scale
    for o_ref in out_refs:
        o_ref[...] = acc.astype(o_ref.dtype)


def _mm(x, w, out_dtypes, scale=None, tm_pref=1024, tn_pref=512):
    m, k = x.shape
    n = w.shape[1]
    tm, tn = _tile(m, tm_pref), _tile(n, tn_pref)
    cast_x = x.dtype != BF16
    kern = functools.partial(_mm_kernel, scale=scale, cast_x=cast_x, n_out=len(out_dtypes))
    outs = pl.pallas_call(
        kern,
        grid=(m // tm, n // tn),
        in_specs=[pl.BlockSpec((tm, k), lambda i, j: (i, 0)),
                  pl.BlockSpec((k, tn), lambda i, j: (0, j))],
        out_specs=[pl.BlockSpec((tm, tn), lambda i, j: (i, j)) for _ in out_dtypes],
        out_shape=[jax.ShapeDtypeStruct((m, n), dt) for dt in out_dtypes],
        scratch_shapes=[pltpu.VMEM((tm, k), BF16)] if cast_x else [],
        compiler_params=_params("parallel", "arbitrary"),
        name="mm",
    )(x, w)
    return outs


def _mm_ln_kernel(x_ref, w_ref, res_ref, g_ref, b_ref, of_ref, ob_ref, *, alpha):
    f = jnp.dot(x_ref[...], w_ref[...], preferred_element_type=F32)
    y = _layer_norm(alpha * res_ref[...] + f, g_ref[...], b_ref[...])
    of_ref[...] = y
    ob_ref[...] = y.astype(BF16)


def _mm_ln(x, w, res, g, b, alpha, tm_pref=256):
    m, k = x.shape
    d = w.shape[1]
    tm = _tile(m, tm_pref)
    return pl.pallas_call(
        functools.partial(_mm_ln_kernel, alpha=alpha),
        grid=(m // tm,),
        in_specs=[pl.BlockSpec((tm, k), lambda i: (i, 0)),
                  pl.BlockSpec((k, d), lambda i: (0, 0)),
                  pl.BlockSpec((tm, d), lambda i: (i, 0)),
                  pl.BlockSpec((1, d), lambda i: (0, 0)),
                  pl.BlockSpec((1, d), lambda i: (0, 0))],
        out_specs=[pl.BlockSpec((tm, d), lambda i: (i, 0)),
                   pl.BlockSpec((tm, d), lambda i: (i, 0))],
        out_shape=[jax.ShapeDtypeStruct((m, d), F32), jax.ShapeDtypeStruct((m, d), BF16)],
        compiler_params=_params("parallel"),
        name="mm_ln",
    )(x, w, res, g, b)


def _ffn_kernel(x_ref, wg_ref, wu_ref, wd_ref, res_ref, g_ref, b_ref, of_ref, ob_ref, acc_ref, *, alpha):
    f = pl.program_id(1)

    @pl.when(f == 0)
    def _():
        acc_ref[...] = jnp.zeros_like(acc_ref)

    x = x_ref[...]
    gate = jnp.dot(x, wg_ref[...], preferred_element_type=F32)
    up = jnp.dot(x, wu_ref[...], preferred_element_type=F32)
    h = (gate * jax.nn.sigmoid(gate) * up).astype(BF16)
    acc_ref[...] += jnp.dot(h, wd_ref[...], preferred_element_type=F32)

    @pl.when(f == pl.num_programs(1) - 1)
    def _():
        y = _layer_norm(alpha * res_ref[...] + acc_ref[...], g_ref[...], b_ref[...])
        of_ref[...] = y
        ob_ref[...] = y.astype(BF16)


def _ffn(x, wg, wu, wd, res, g, b, alpha, tm_pref=512, tf_pref=512):
    m, d = x.shape
    ff = wg.shape[1]
    tm, tf = _tile(m, tm_pref), _tile(ff, tf_pref)
    return pl.pallas_call(
        functools.partial(_ffn_kernel, alpha=alpha),
        grid=(m // tm, ff // tf),
        in_specs=[pl.BlockSpec((tm, d), lambda i, f: (i, 0)),
                  pl.BlockSpec((d, tf), lambda i, f: (0, f)),
                  pl.BlockSpec((d, tf), lambda i, f: (0, f)),
                  pl.BlockSpec((tf, d), lambda i, f: (f, 0)),
                  pl.BlockSpec((tm, d), lambda i, f: (i, 0)),
                  pl.BlockSpec((1, d), lambda i, f: (0, 0)),
                  pl.BlockSpec((1, d), lambda i, f: (0, 0))],
        out_specs=[pl.BlockSpec((tm, d), lambda i, f: (i, 0)),
                   pl.BlockSpec((tm, d), lambda i, f: (i, 0))],
        out_shape=[jax.ShapeDtypeStruct((m, d), F32), jax.ShapeDtypeStruct((m, d), BF16)],
        scratch_shapes=[pltpu.VMEM((tm, d), F32)],
        compiler_params=_params("parallel", "arbitrary"),
        name="ffn",
    )(x, wg, wu, wd, res, g, b)


def _evenmix_kernel(qkv_ref, rest_ref, gate_ref, bg_ref, cw_ref,
                    mix_ref, c_out, n_out, m_out, cv_out,
                    c_s, n_s, m_s, u_s, *, nh, dk, dv, dc, chunk):
    c = pl.program_id(1)
    last = pl.num_programs(1) - 1
    dm = nh * dk
    L = chunk

    @pl.when(c == 0)
    def _():
        c_s[...] = jnp.zeros_like(c_s)
        n_s[...] = jnp.zeros_like(n_s)
        m_s[...] = jnp.zeros_like(m_s)
        u_s[...] = jnp.zeros_like(u_s)

    g = gate_ref[...] + bg_ref[...]
    lane = lax.broadcasted_iota(jnp.int32, (L, LANES), 1)
    row = lax.broadcasted_iota(jnp.int32, (L, LANES), 0)
    bcum = _log_sigmoid(g)
    s = 1
    while s < L:
        bcum = bcum + jnp.where(row >= s, pltpu.roll(bcum, s, 0), 0.0)
        s *= 2
    z = jnp.where(lane < nh, g, bcum)
    zt = z.T

    tr = lax.broadcasted_iota(jnp.int32, (L, L), 0)
    tc = lax.broadcasted_iota(jnp.int32, (L, L), 1)
    causal = tr >= tc
    lane1 = lax.broadcasted_iota(jnp.int32, (1, LANES), 1)
    m_vec = jnp.zeros((1, LANES), F32)

    for h in range(nh):
        qc = qkv_ref[:, h * dk:(h + 1) * dk]
        kc = qkv_ref[:, dm + h * dk:dm + (h + 1) * dk] * (dk ** -0.5)
        vc = qkv_ref[:, 2 * dm + h * dv:2 * dm + (h + 1) * dv]
        ig_col = z[:, h:h + 1]
        b_col = z[:, nh + h:nh + h + 1]
        ig_row = zt[h:h + 1, :]
        b_row = zt[nh + h:nh + h + 1, :]
        m_prev = m_s[h][:, 0:1]
        cmat = c_s[h]
        n_row = n_s[h]

        dmat = jnp.where(causal, b_col - b_row + ig_row, -jnp.inf)
        inter = b_col + m_prev
        m_t = jnp.maximum(inter, jnp.max(dmat, axis=-1, keepdims=True))
        w_intra = jnp.exp(dmat - m_t)
        w_inter = jnp.exp(inter - m_t)
        sc = lax.dot_general(qc, kc, (((1,), (1,)), ((), ())), preferred_element_type=F32) * w_intra
        num = (w_inter * jnp.dot(qc, cmat.astype(BF16), preferred_element_type=F32)
               + jnp.dot(sc.astype(BF16), vc, preferred_element_type=F32))
        qn = jnp.sum(qc.astype(F32) * n_row, axis=-1, keepdims=True)
        den = w_inter * qn + jnp.sum(sc, axis=-1, keepdims=True)
        out = num / jnp.maximum(jnp.abs(den), jnp.exp(-m_t))

        m_new = m_t[L - 1:L, :]
        b_last = b_col[L - 1:L, :]
        g_inter = jnp.exp(b_last + m_prev - m_new)
        g_s = jnp.exp(b_last - b_col + ig_col - m_new)
        kg = kc.astype(F32) * g_s
        c_s[h] = g_inter * cmat + jnp.dot(kg.T.astype(BF16), vc, preferred_element_type=F32)
        n_s[h] = g_inter * n_row + jnp.sum(kg, axis=0, keepdims=True)
        m_s[h] = jnp.broadcast_to(m_new, (1, LANES))
        m_vec = jnp.where(lane1 == h, m_new, m_vec)

        o_gate = rest_ref[:, h * dv:(h + 1) * dv]
        mix_ref[:, h * dv:(h + 1) * dv] = (jax.nn.sigmoid(o_gate) * out).astype(BF16)

    base = nh * dv
    gb = rest_ref[:, base:base + dc]
    u = rest_ref[:, base + dc:base + 2 * dc] * rest_ref[:, base + 2 * dc:base + 3 * dc]
    prev2 = u_s[0:1, :]
    prev1 = u_s[1:2, :]
    rowc = lax.broadcasted_iota(jnp.int32, (L, dc), 0)
    u1 = jnp.where(rowc == 0, prev1, pltpu.roll(u, 1, 0))
    u2 = jnp.where(rowc == 0, prev2, jnp.where(rowc == 1, prev1, pltpu.roll(u, 2, 0)))
    yc = gb * (cw_ref[0:1, :] * u2 + cw_ref[1:2, :] * u1 + cw_ref[2:3, :] * u)
    mix_ref[:, dm:dm + dc] = yc.astype(BF16)
    u_s[0:1, :] = u[L - 2:L - 1, :]
    u_s[1:2, :] = u[L - 1:L, :]

    @pl.when(c == last)
    def _():
        c_out[0] = c_s[...]
        n_out[0] = n_s[...]
        m_out[0] = m_vec
        cv_out[0, 0:1, :] = u[L - 2:L - 1, :]
        cv_out[0, 1:2, :] = u[L - 1:L, :]


def _evenmix_prompt(qkv, rest, gates, b_gate_pad, conv_w, bsz, t, nh, dk, dv, dc, chunk):
    nc = t // chunk
    dm = nh * dk
    kern = functools.partial(_evenmix_kernel, nh=nh, dk=dk, dv=dv, dc=dc, chunk=chunk)
    return pl.pallas_call(
        kern,
        grid=(bsz, nc),
        in_specs=[pl.BlockSpec((chunk, 3 * dm), lambda b, c: (b * nc + c, 0)),
                  pl.BlockSpec((chunk, dm + 3 * dc), lambda b, c: (b * nc + c, 0)),
                  pl.BlockSpec((chunk, LANES), lambda b, c: (b * nc + c, 0)),
                  pl.BlockSpec((1, LANES), lambda b, c: (0, 0)),
                  pl.BlockSpec((CONV_W, dc), lambda b, c: (0, 0))],
        out_specs=[pl.BlockSpec((chunk, dm + dc), lambda b, c: (b * nc + c, 0)),
                   pl.BlockSpec((1, nh, dk, dv), lambda b, c: (b, 0, 0, 0)),
                   pl.BlockSpec((1, nh, 1, dk), lambda b, c: (b, 0, 0, 0)),
                   pl.BlockSpec((1, 1, LANES), lambda b, c: (b, 0, 0)),
                   pl.BlockSpec((1, CONV_W - 1, dc), lambda b, c: (b, 0, 0))],
        out_shape=[jax.ShapeDtypeStruct((bsz * t, dm + dc), BF16),
                   jax.ShapeDtypeStruct((bsz, nh, dk, dv), F32),
                   jax.ShapeDtypeStruct((bsz, nh, 1, dk), F32),
                   jax.ShapeDtypeStruct((bsz, 1, LANES), F32),
                   jax.ShapeDtypeStruct((bsz, CONV_W - 1, dc), F32)],
        scratch_shapes=[pltpu.VMEM((nh, dk, dv), F32),
                        pltpu.VMEM((nh, 1, dk), F32),
                        pltpu.VMEM((nh, 1, LANES), F32),
                        pltpu.VMEM((SUBLANES, dc), F32)],
        compiler_params=_params("parallel", "arbitrary"),
        name="evenmix_prompt",
    )(qkv, rest, gates, b_gate_pad, conv_w)


def _evenmix_step_kernel(qcol_ref, kcol_ref, qrow_ref, krow_ref, vrow_ref, orow_ref, conv_ref,
                         gate_ref, bg_ref, cw_ref, c_ref, n_ref, m_ref, cv_ref,
                         mix_ref, c_out, n_out, m_out, cv_out, *, nh, dk, dv, dc):
    dm = nh * dk
    g = gate_ref[0] + bg_ref[...]
    lf_all = _log_sigmoid(g)
    lane1 = lax.broadcasted_iota(jnp.int32, (1, LANES), 1)
    m_vec = jnp.zeros((1, LANES), F32)
    scale = dk ** -0.5
    for h in range(nh):
        q_col = qcol_ref[0, h]
        k_col = kcol_ref[0, h] * scale
        q_row = qrow_ref[0, h]
        k_row = krow_ref[0, h] * scale
        v_row = vrow_ref[0, h]
        ig = g[:, h:h + 1]
        lf = lf_all[:, nh + h:nh + h + 1]
        m_prev = m_ref[0][:, h:h + 1]
        cmat = c_ref[0, h]
        n_row = n_ref[0, h]

        inter = lf + m_prev
        m_t = jnp.maximum(inter, ig)
        w_intra = jnp.exp(ig - m_t)
        w_inter = jnp.exp(inter - m_t)
        sc = jnp.sum(q_row * k_row, axis=-1, keepdims=True) * w_intra
        num = w_inter * jnp.sum(q_col * cmat, axis=0, keepdims=True) + sc * v_row
        den = w_inter * jnp.sum(q_row * n_row, axis=-1, keepdims=True) + sc
        out = num / jnp.maximum(jnp.abs(den), jnp.exp(-m_t))

        g_inter = jnp.exp(lf + m_prev - m_t)
        g_s = jnp.exp(ig - m_t)
        c_out[0, h] = g_inter * cmat + (g_s * k_col) * v_row
        n_out[0, h] = g_inter * n_row + g_s * k_row
        m_vec = jnp.where(lane1 == h, m_t, m_vec)
        mix_ref[0, :, h * dv:(h + 1) * dv] = jax.nn.sigmoid(orow_ref[0, h]) * out
    m_out[0] = m_vec

    gb = conv_ref[0, 0:1, :]
    u = conv_ref[0, 1:2, :] * conv_ref[0, 2:3, :]
    prev2 = cv_ref[0, 0:1, :]
    prev1 = cv_ref[0, 1:2, :]
    yc = gb * (cw_ref[0:1, :] * prev2 + cw_ref[1:2, :] * prev1 + cw_ref[2:3, :] * u)
    mix_ref[0, :, dm:dm + dc] = yc
    cv_out[0, 0:1, :] = prev1
    cv_out[0, 1:2, :] = u


def _evenmix_step(q, k, v, o, conv_in, gates, b_gate_pad, conv_w, c0, n0, m0_pad, cv0, nh, dk, dv, dc):
    bsz = q.shape[0]
    dm = nh * dk
    qcol = q.reshape(bsz, nh, dk, 1)
    kcol = k.reshape(bsz, nh, dk, 1)
    qrow = q.reshape(bsz, nh, 1, dk)
    krow = k.reshape(bsz, nh, 1, dk)
    vrow = v.reshape(bsz, nh, 1, dv)
    orow = o.reshape(bsz, nh, 1, dv)
    col_spec = pl.BlockSpec((1, nh, dk, 1), lambda b: (b, 0, 0, 0))
    row_spec = pl.BlockSpec((1, nh, 1, dk), lambda b: (b, 0, 0, 0))
    vrow_spec = pl.BlockSpec((1, nh, 1, dv), lambda b: (b, 0, 0, 0))
    kern = functools.partial(_evenmix_step_kernel, nh=nh, dk=dk, dv=dv, dc=dc)
    return pl.pallas_call(
        kern,
        grid=(bsz,),
        in_specs=[col_spec, col_spec, row_spec, row_spec, vrow_spec, vrow_spec,
                  pl.BlockSpec((1, 3, dc), lambda b: (b, 0, 0)),
                  pl.BlockSpec((1, 1, LANES), lambda b: (b, 0, 0)),
                  pl.BlockSpec((1, LANES), lambda b: (0, 0)),
                  pl.BlockSpec((CONV_W, dc), lambda b: (0, 0)),
                  pl.BlockSpec((1, nh, dk, dv), lambda b: (b, 0, 0, 0)),
                  pl.BlockSpec((1, nh, 1, dk), lambda b: (b, 0, 0, 0)),
                  pl.BlockSpec((1, 1, LANES), lambda b: (b, 0, 0)),
                  pl.BlockSpec((1, CONV_W - 1, dc), lambda b: (b, 0, 0))],
        out_specs=[pl.BlockSpec((1, 1, dm + dc), lambda b: (b, 0, 0)),
                   pl.BlockSpec((1, nh, dk, dv), lambda b: (b, 0, 0, 0)),
                   pl.BlockSpec((1, nh, 1, dk), lambda b: (b, 0, 0, 0)),
                   pl.BlockSpec((1, 1, LANES), lambda b: (b, 0, 0)),
                   pl.BlockSpec((1, CONV_W - 1, dc), lambda b: (b, 0, 0))],
        out_shape=[jax.ShapeDtypeStruct((bsz, 1, dm + dc), F32),
                   jax.ShapeDtypeStruct((bsz, nh, dk, dv), F32),
                   jax.ShapeDtypeStruct((bsz, nh, 1, dk), F32),
                   jax.ShapeDtypeStruct((bsz, 1, LANES), F32),
                   jax.ShapeDtypeStruct((bsz, CONV_W - 1, dc), F32)],
        compiler_params=_params("parallel"),
        name="evenmix_step",
    )(qcol, kcol, qrow, krow, vrow, orow, conv_in, gates, b_gate_pad, conv_w, c0, n0, m0_pad, cv0)


def _diff_lambda(lamv_ref, lam_init):
    a = jnp.sum(lamv_ref[0:1, :] * lamv_ref[1:2, :], axis=-1, keepdims=True)
    b = jnp.sum(lamv_ref[2:3, :] * lamv_ref[3:4, :], axis=-1, keepdims=True)
    return jnp.exp(a) - jnp.exp(b) + lam_init


def _attn_prompt_kernel(slope_ref, q_ref, k_ref, v_ref, lamv_ref, sg_ref, o_ref,
                        m_s, l_s, acc_s, kb_s, sa_s, sb_s, *, dh, dv, blk, lam_init):
    h = pl.program_id(1)
    qi = pl.program_id(2)
    t_all = k_ref.shape[0]
    nslab = blk // LANES
    nhalf = dv // LANES

    @pl.when(qi == 0)
    def _():
        pos = (lax.broadcasted_iota(jnp.int32, (t_all, LANES), 0).astype(F32)
               * (slope_ref[h] * LOG2E))
        lane = lax.broadcasted_iota(jnp.int32, (t_all, LANES), 1)
        hi = pos.astype(BF16).astype(F32)
        r1 = pos - hi
        mid = r1.astype(BF16).astype(F32)
        lo = r1 - mid
        pieces = jnp.where(lane == 0, hi, jnp.where(lane == 1, mid, jnp.where(lane == 2, lo, 0.0)))
        kb_s[...] = pieces.astype(BF16)

    m_s[...] = jnp.full_like(m_s, -jnp.inf)
    l_s[...] = jnp.zeros_like(l_s)
    acc_s[...] = jnp.zeros_like(acc_s)

    def scores(ki, s_ref):
        start = pl.multiple_of(ki * blk, blk)
        lane_q = lax.broadcasted_iota(jnp.int32, (blk, LANES), 1)
        ones_cols = jnp.where(lane_q < 3, 1.0, 0.0).astype(BF16)
        kb = kb_s[pl.ds(start, blk), :]
        for c in range(2):
            qa = jnp.concatenate([q_ref[:, c * dh:(c + 1) * dh], ones_cols], axis=1)
            ka = jnp.concatenate([k_ref[pl.ds(start, blk), c * dh:(c + 1) * dh], kb], axis=1)
            s_ref[c] = lax.dot_general(qa, ka, (((1,), (1,)), ((), ())), preferred_element_type=F32)

    def accumulate(ki, s_ref, masked):
        start = pl.multiple_of(ki * blk, blk)
        v = v_ref[pl.ds(start, blk), :]
        for c in range(2):
            s = s_ref[c]
            if masked:
                row = lax.broadcasted_iota(jnp.int32, (blk, blk), 0)
                col = lax.broadcasted_iota(jnp.int32, (blk, blk), 1)
                s = jnp.where(row >= col, s, -jnp.inf)
            slabs = [s[:, j * LANES:(j + 1) * LANES] for j in range(nslab)]
            mx = functools.reduce(jnp.maximum, slabs)
            m_prev = m_s[c]
            m_new = jnp.maximum(m_prev, jnp.max(mx, axis=-1, keepdims=True))
            alpha = jnp.exp2(m_prev - m_new)
            ps = [jnp.exp2(sl - m_new) for sl in slabs]
            psum = functools.reduce(lambda a, b: a + b, ps)
            l_s[c] = alpha * l_s[c] + jnp.sum(psum, axis=-1, keepdims=True)
            m_s[c] = m_new
            pv = jnp.dot(jnp.concatenate([x.astype(BF16) for x in ps], axis=1), v,
                         preferred_element_type=F32)
            for e in range(nhalf):
                sl = slice(e * LANES, (e + 1) * LANES)
                acc_s[c, :, sl] = alpha * acc_s[c, :, sl] + pv[:, sl]

    npairs = qi // 2
    scores(0, sa_s)

    def body(j, carry):
        scores(2 * j + 1, sb_s)
        accumulate(2 * j, sa_s, masked=False)
        scores(2 * j + 2, sa_s)
        accumulate(2 * j + 1, sb_s, masked=False)
        return carry

    lax.fori_loop(0, npairs, body, 0)

    @pl.when(qi == 2 * npairs)
    def _():
        accumulate(qi, sa_s, masked=True)

    @pl.when(qi != 2 * npairs)
    def _():
        scores(qi, sb_s)
        accumulate(qi - 1, sa_s, masked=False)
        accumulate(qi, sb_s, masked=True)

    lam = _diff_lambda(lamv_ref, lam_init)
    inv0 = 1.0 / l_s[0]
    inv1 = lam / l_s[1]
    halves = [acc_s[0, :, e * LANES:(e + 1) * LANES] * inv0 - acc_s[1, :, e * LANES:(e + 1) * LANES] * inv1
              for e in range(nhalf)]
    ssq = functools.reduce(lambda a, b: a + b, [jnp.sum(o * o, axis=-1, keepdims=True) for o in halves])
    inv = lax.rsqrt(ssq / dv + LN_EPS) * (1.0 - lam_init)
    for e in range(nhalf):
        sl = slice(e * LANES, (e + 1) * LANES)
        o_ref[:, sl] = (halves[e] * inv * sg_ref[:, sl]).astype(BF16)


def _attn_prompt(q, k, v, slopes, lamv, subln_g, bsz, t, nh, dh, dv, lam_init, blk_pref=512):
    blk = _tile(t, blk_pref)
    nq = t // blk
    kern = functools.partial(_attn_prompt_kernel, dh=dh, dv=dv, blk=blk, lam_init=lam_init)
    return pl.pallas_call(
        kern,
        grid=(bsz, nh, nq),
        in_specs=[pl.BlockSpec(memory_space=pltpu.SMEM),
                  pl.BlockSpec((blk, 2 * dh), lambda b, h, qi: (b * nq + qi, h)),
                  pl.BlockSpec((t, 2 * dh), lambda b, h, qi: (b, h)),
                  pl.BlockSpec((t, dv), lambda b, h, qi: (b, h)),
                  pl.BlockSpec((4, dh), lambda b, h, qi: (0, 0)),
                  pl.BlockSpec((1, dv), lambda b, h, qi: (0, 0))],
        out_specs=pl.BlockSpec((blk, dv), lambda b, h, qi: (b * nq + qi, h)),
        out_shape=jax.ShapeDtypeStruct((bsz * t, nh * dv), BF16),
        scratch_shapes=[pltpu.VMEM((2, blk, LANES), F32),
                        pltpu.VMEM((2, blk, LANES), F32),
                        pltpu.VMEM((2, blk, dv), F32),
                        pltpu.VMEM((t, LANES), BF16),
                        pltpu.VMEM((2, blk, blk), F32),
                        pltpu.VMEM((2, blk, blk), F32)],
        compiler_params=_params("parallel", "parallel", "arbitrary"),
        name="attn_prompt",
    )(slopes, q, k, v, lamv, subln_g)


def _attn_step_kernel(pt_ref, q_ref, kn_ref, vn_ref, slope_ref, lamv_ref, sg_ref, *refs,
                      dh, dv, page, pages_per_step, past, lam_init, group):
    del pt_ref
    npg = pages_per_step
    k_refs = refs[:npg]
    v_refs = refs[npg:2 * npg]
    o_ref = refs[2 * npg]
    m_s, l_s, acc_s, pos_s, s_s = refs[2 * npg + 1:]
    pp = pl.program_id(1)
    nh = q_ref.shape[1]
    nhalf = dv // LANES
    ones = jnp.ones((dh, LANES), F32)
    slope = slope_ref[...]

    def lane_sum(x2d):
        return jnp.dot(x2d, ones, preferred_element_type=F32)

    @pl.when(pp == 0)
    def _():
        pos_s[...] = lax.broadcasted_iota(jnp.int32, (page, nh, LANES), 0).astype(F32) * slope[None]
        for c in range(2):
            s_self = lane_sum(q_ref[0, :, c * dh:(c + 1) * dh] * kn_ref[0, :, c * dh:(c + 1) * dh])
            m_s[c] = s_self
            l_s[c] = jnp.ones((nh, LANES), F32)
            for e in range(nhalf):
                acc_s[c, e] = vn_ref[0, :, e * LANES:(e + 1) * LANES]

    page_bias = [slope * ((pp * npg + i) * page - past).astype(F32) for i in range(npg)]
    mx = [None, None]
    for i in range(npg):
        for c in range(2):
            prod = k_refs[i][:, :, c * dh:(c + 1) * dh] * q_ref[0, :, c * dh:(c + 1) * dh][None]
            s = lane_sum(prod.reshape(page * nh, dh)).reshape(page, nh, LANES) + pos_s[...]
            s_s[i * 2 + c] = s
            cur = jnp.max(s, axis=0) + page_bias[i]
            mx[c] = cur if mx[c] is None else jnp.maximum(mx[c], cur)
    alpha, shift = [], []
    for c in range(2):
        m_prev = m_s[c]
        m_new = jnp.maximum(m_prev, mx[c])
        alpha.append(jnp.exp2(m_prev - m_new))
        shift.append([m_new - page_bias[i] for i in range(npg)])
        m_s[c] = m_new

    zero = jnp.zeros((nh, LANES), F32)
    sums = (zero,) * (2 + 2 * nhalf)
    for i in range(npg):
        def body(g, carry, i=i):
            t0 = pl.multiple_of(g * group, group)
            vs = [v_refs[i][pl.ds(t0, group), :, e * LANES:(e + 1) * LANES] for e in range(nhalf)]
            new = list(carry)
            for c in range(2):
                p = jnp.exp2(s_s[i * 2 + c, pl.ds(t0, group)] - shift[c][i][None])
                new[c] = new[c] + jnp.sum(p, axis=0)
                for e in range(nhalf):
                    new[2 + c * nhalf + e] = new[2 + c * nhalf + e] + jnp.sum(p * vs[e], axis=0)
            return tuple(new)

        sums = lax.fori_loop(0, page // group, body, sums, unroll=2)
    for c in range(2):
        l_s[c] = alpha[c] * l_s[c] + sums[c]
        for e in range(nhalf):
            acc_s[c, e] = alpha[c] * acc_s[c, e] + sums[2 + c * nhalf + e]

    @pl.when(pp == pl.num_programs(1) - 1)
    def _():
        lam = _diff_lambda(lamv_ref, lam_init)
        halves = [acc_s[0, e] / l_s[0] - lam * (acc_s[1, e] / l_s[1]) for e in range(nhalf)]
        ssq = sum(jnp.sum(o * o, axis=-1, keepdims=True) for o in halves)
        inv = lax.rsqrt(ssq / dv + LN_EPS) * (1.0 - lam_init)
        for e in range(nhalf):
            o_ref[0, :, e * LANES:(e + 1) * LANES] = halves[e] * inv * sg_ref[:, e * LANES:(e + 1) * LANES]


def _attn_step(q, kn, vn, cache_k, cache_v, layer_j, page_table, slopes_rep, lamv, subln_g,
               nh, dh, dv, lam_init, pages_per_step=4):
    bsz, n_pages = page_table.shape
    page = cache_k.shape[2]
    past = n_pages * page
    pps = pages_per_step if n_pages % pages_per_step == 0 else 1
    kern = functools.partial(_attn_step_kernel, dh=dh, dv=dv, page=page, pages_per_step=pps,
                             past=past, lam_init=lam_init, group=math.gcd(page, SUBLANES))

    def page_spec(i, width):
        return pl.BlockSpec((None, None, page, nh, width),
                            lambda b, pp, pt: (layer_j, pt[b * n_pages + pp * pps + i], 0, 0, 0))

    tok_spec = lambda width: pl.BlockSpec((1, nh, width), lambda b, pp, pt: (b, 0, 0))
    grid_spec = pltpu.PrefetchScalarGridSpec(
        num_scalar_prefetch=1,
        grid=(bsz, n_pages // pps),
        in_specs=[tok_spec(2 * dh), tok_spec(2 * dh), tok_spec(dv),
                  pl.BlockSpec((nh, LANES), lambda b, pp, pt: (0, 0)),
                  pl.BlockSpec((4, dh), lambda b, pp, pt: (0, 0)),
                  pl.BlockSpec((1, dv), lambda b, pp, pt: (0, 0))]
                 + [page_spec(i, 2 * dh) for i in range(pps)]
                 + [page_spec(i, dv) for i in range(pps)],
        out_specs=tok_spec(dv),
        scratch_shapes=[pltpu.VMEM((2, nh, LANES), F32),
                        pltpu.VMEM((2, nh, LANES), F32),
                        pltpu.VMEM((2, dv // LANES, nh, LANES), F32),
                        pltpu.VMEM((page, nh, LANES), F32),
                        pltpu.VMEM((2 * pps, page, nh, LANES), F32)],
    )
    return pl.pallas_call(
        kern,
        grid_spec=grid_spec,
        out_shape=jax.ShapeDtypeStruct((bsz, nh, dv), F32),
        compiler_params=_params("parallel", "arbitrary"),
        name="attn_step",
    )(page_table.reshape(-1), q, kn, vn, slopes_rep, lamv, subln_g,
      *([cache_k] * pps), *([cache_v] * pps))


def kernel(x_prompt, x_sample, state_mlstm_c, state_mlstm_n, state_mlstm_m, state_conv, cache_k, cache_v, page_table, w_in_even, b_gate_even, conv_w_even, w_out_even, w_qkv_odd, w_o_odd, lambda_q1, lambda_k1, lambda_q2, lambda_k2, subln_g, w_ffn_gate, w_ffn_up, w_ffn_down, ln_mix_g, ln_mix_b, ln_ffn_g, ln_ffn_b):
    bp, t, d = x_prompt.shape
    bs = x_sample.shape[0]
    assert x_sample.shape[1] == 1, "the sample group decodes one token per sequence"
    depth = w_ffn_gate.shape[0]
    nh_a, dk_a, dv_a = state_mlstm_c.shape[2:]
    dm = nh_a * dk_a
    dc = state_conv.shape[-1]
    nh_c, dv_c = cache_v.shape[3:]
    dh_c = cache_k.shape[4] // 2
    alpha = (2 * depth) ** 0.25
    chunk = math.gcd(t, 128)

    xp = x_prompt.reshape(bp * t, d)
    xs = x_sample.reshape(bs, d)
    xp_b = xs_b = None
    slopes = jnp.exp2(-8.0 * jnp.arange(1, nh_c + 1, dtype=F32) / nh_c)
    slopes_rep = jnp.broadcast_to((slopes * LOG2E)[:, None], (nh_c, LANES))

    pc, pn, pm, pconv, pk, pv = [], [], [], [], [], []
    sc, sn, sm, sconv, sk, sv = [], [], [], [], [], []
    for layer in range(depth):
        j = layer // 2
        g_mix, b_mix = ln_mix_g[layer][None], ln_mix_b[layer][None]
        g_ffn, b_ffn = ln_ffn_g[layer][None], ln_ffn_b[layer][None]
        if layer % 2 == 0:
            w_in = w_in_even[j]
            w_qkv = w_in[:, :3 * dm].astype(BF16)
            gate_lo = 4 * dm
            w_gate = jnp.pad(w_in[:, gate_lo:gate_lo + 2 * nh_a], ((0, 0), (0, LANES - 2 * nh_a))).astype(BF16)
            w_rest = jnp.concatenate([w_in[:, 3 * dm:4 * dm], w_in[:, gate_lo + 2 * nh_a:]], axis=1).astype(BF16)
            w_out = w_out_even[j].astype(BF16)
            b_gate_pad = jnp.pad(b_gate_even[j], (0, LANES - 2 * nh_a))[None]
            conv_w = conv_w_even[j]

            x_in = xp if xp_b is None else xp_b
            (qkv,) = _mm(x_in, w_qkv, [BF16])
            (rest,) = _mm(x_in, w_rest, [F32])
            (gates,) = _mm(x_in, w_gate, [F32])
            mix, c1, n1, m1, cv1 = _evenmix_prompt(qkv, rest, gates, b_gate_pad, conv_w,
                                                   bp, t, nh_a, dk_a, dv_a, dc, chunk)
            xp, xp_b = _mm_ln(mix, w_out, xp, g_mix, b_mix, alpha)
            pc.append(c1); pn.append(n1.reshape(bp, nh_a, dk_a)); pm.append(m1[:, 0, :nh_a]); pconv.append(cv1)

            x_in = xs if xs_b is None else xs_b
            (qkv,) = _mm(x_in, w_qkv, [F32])
            (rest,) = _mm(x_in, w_rest, [F32])
            (gates,) = _mm(x_in, w_gate, [F32])
            conv_in = rest[:, dm:].reshape(bs, 3, dc)
            m0_pad = jnp.pad(state_mlstm_m[j], ((0, 0), (0, LANES - nh_a)))[:, None, :]
            mix, c2, n2, m2, cv2 = _evenmix_step(
                qkv[:, :dm], qkv[:, dm:2 * dm], qkv[:, 2 * dm:], rest[:, :dm], conv_in,
                gates[:, None, :], b_gate_pad, conv_w, state_mlstm_c[j],
                state_mlstm_n[j][:, :, None, :], m0_pad, state_conv[j], nh_a, dk_a, dv_a, dc)
            xs, xs_b = _mm_ln(mix.reshape(bs, dm + dc).astype(BF16), w_out, xs, g_mix, b_mix, alpha)
            sc.append(c2); sn.append(n2.reshape(bs, nh_a, dk_a)); sm.append(m2[:, 0, :nh_a]); sconv.append(cv2)
        else:
            lam_init = 0.8 - 0.6 * math.exp(-0.3 * layer)
            d_attn = nh_c * dv_c
            w_qkv = w_qkv_odd[j]
            w_q = w_qkv[:, :d_attn].astype(BF16)
            w_k = w_qkv[:, d_attn:2 * d_attn].astype(BF16)
            w_v = w_qkv[:, 2 * d_attn:].astype(BF16)
            w_o = w_o_odd[j].astype(BF16)
            lamv = jnp.stack([lambda_q1[j], lambda_k1[j], lambda_q2[j], lambda_k2[j]])
            sg = subln_g[j][None]
            q_scale = dh_c ** -0.5 * LOG2E

            x_in = xp if xp_b is None else xp_b
            (q_b,) = _mm(x_in, w_q, [BF16], scale=q_scale)
            k_f, k_b = _mm(x_in, w_k, [F32, BF16])
            v_f, v_b = _mm(x_in, w_v, [F32, BF16])
            att = _attn_prompt(q_b, k_b, v_b, slopes, lamv, sg, bp, t, nh_c, dh_c, dv_c, lam_init)
            xp, xp_b = _mm_ln(att, w_o, xp, g_mix, b_mix, alpha)
            pk.append(k_f.reshape(bp, t, nh_c, 2 * dh_c)); pv.append(v_f.reshape(bp, t, nh_c, dv_c))

            x_in = xs if xs_b is None else xs_b
            (q_s,) = _mm(x_in, w_q, [F32], scale=q_scale)
            (k_s,) = _mm(x_in, w_k, [F32])
            (v_s,) = _mm(x_in, w_v, [F32])
            att = _attn_step(q_s.reshape(bs, nh_c, 2 * dh_c), k_s.reshape(bs, nh_c, 2 * dh_c),
                             v_s.reshape(bs, nh_c, dv_c), cache_k, cache_v, j, page_table,
                             slopes_rep, lamv, sg, nh_c, dh_c, dv_c, lam_init)
            xs, xs_b = _mm_ln(att.reshape(bs, d_attn).astype(BF16), w_o, xs, g_mix, b_mix, alpha)
            sk.append(k_s.reshape(bs, 1, nh_c, 2 * dh_c)); sv.append(v_s.reshape(bs, 1, nh_c, dv_c))

        wg = w_ffn_gate[layer].astype(BF16)
        wu = w_ffn_up[layer].astype(BF16)
        wd = w_ffn_down[layer].astype(BF16)
        xp, xp_b = _ffn(xp_b, wg, wu, wd, xp, g_ffn, b_ffn, alpha)
        xs, xs_b = _ffn(xs_b, wg, wu, wd, xs, g_ffn, b_ffn, alpha)

    return (xp.reshape(bp, t, d), xs.reshape(bs, 1, d),
            jnp.stack(pc), jnp.stack(pn), jnp.stack(pm), jnp.stack(pconv), jnp.stack(pk), jnp.stack(pv),
            jnp.stack(sc), jnp.stack(sn), jnp.stack(sm), jnp.stack(sconv), jnp.stack(sk), jnp.stack(sv))
```

```python
import functools
import math

import jax
import jax.numpy as jnp
from jax import lax
from jax.experimental import pallas as pl
from jax.experimental.pallas import tpu as pltpu

F32 = jnp.float32
BF16 = jnp.bfloat16

LN_EPS = 1e-5
CONV_W = 3
LOG2E = math.log2(math.e)
V7X_VMEM_LIMIT_BYTES = 56 * 1024 * 1024
LANES = 128
SUBLANES = 8


def _params(*sem):
    return pltpu.CompilerParams(dimension_semantics=sem, vmem_limit_bytes=V7X_VMEM_LIMIT_BYTES)


def _layer_norm(y, g, b):
    mu = jnp.mean(y, axis=-1, keepdims=True)
    d = y - mu
    var = jnp.mean(d * d, axis=-1, keepdims=True)
    return d * lax.rsqrt(var + LN_EPS) * g + b


def _log_sigmoid(x):
    return jnp.minimum(x, 0.0) - jnp.log1p(jnp.exp(-jnp.abs(x)))


def _tile(n, pref):
    if n <= pref:
        return n
    t = pref
    while n % t:
        t //= 2
    return t


def _mm_kernel(x_ref, w_ref, *refs, scale, cast_x, n_out):
    out_refs = refs[:n_out]
    if cast_x:
        xb_ref = refs[n_out]

        @pl.when(pl.program_id(1) == 0)
        def _():
            xb_ref[...] = x_ref[...].astype(BF16)

        x = xb_ref[...]
    else:
        x = x_ref[...]
    acc = jnp.dot(x, w_ref[...], preferred_element_type=F32)
    if scale is not None:
        acc = acc * scale
    for o_ref in out_refs:
        o_ref[...] = acc.astype(o_ref.dtype)


def _mm(x, w, out_dtypes, scale=None, col0=0, n=None, tm_pref=1024, tn_pref=1024):
    m, k = x.shape
    n = w.shape[1] if n is None else n
    tm, tn = _tile(m, tm_pref), _tile(n, tn_pref)
    assert col0 % tn == 0
    j0 = col0 // tn
    cast_x = x.dtype != BF16
    kern = functools.partial(_mm_kernel, scale=scale, cast_x=cast_x, n_out=len(out_dtypes))
    outs = pl.pallas_call(
        kern,
        grid=(m // tm, n // tn),
        in_specs=[pl.BlockSpec((tm, k), lambda i, j: (i, 0)),
                  pl.BlockSpec((k, tn), lambda i, j: (0, j + j0))],
        out_specs=[pl.BlockSpec((tm, tn), lambda i, j: (i, j)) for _ in out_dtypes],
        out_shape=[jax.ShapeDtypeStruct((m, n), dt) for dt in out_dtypes],
        scratch_shapes=[pltpu.VMEM((tm, k), BF16)] if cast_x else [],
        compiler_params=_params("parallel", "arbitrary"),
        name="mm",
    )(x, w)
    return outs


def _mm_ln_kernel(x_ref, w_ref, res_ref, g_ref, b_ref, of_ref, ob_ref, *, alpha, nsub):
    rows = x_ref.shape[0] // nsub
    for r in range(nsub):
        sl = slice(r * rows, (r + 1) * rows)
        f = jnp.dot(x_ref[sl, :], w_ref[...], preferred_element_type=F32)
        y = _layer_norm(alpha * res_ref[sl, :] + f, g_ref[...], b_ref[...])
        of_ref[sl, :] = y
        ob_ref[sl, :] = y.astype(BF16)


def _mm_ln(x, w, res, g, b, alpha, tm_pref=512):
    m, k = x.shape
    d = w.shape[1]
    tm = _tile(m, tm_pref)
    return pl.pallas_call(
        functools.partial(_mm_ln_kernel, alpha=alpha, nsub=2 if tm % 512 == 0 else 1),
        grid=(m // tm,),
        in_specs=[pl.BlockSpec((tm, k), lambda i: (i, 0)),
                  pl.BlockSpec((k, d), lambda i: (0, 0)),
                  pl.BlockSpec((tm, d), lambda i: (i, 0)),
                  pl.BlockSpec((1, d), lambda i: (0, 0)),
                  pl.BlockSpec((1, d), lambda i: (0, 0))],
        out_specs=[pl.BlockSpec((tm, d), lambda i: (i, 0)),
                   pl.BlockSpec((tm, d), lambda i: (i, 0))],
        out_shape=[jax.ShapeDtypeStruct((m, d), F32), jax.ShapeDtypeStruct((m, d), BF16)],
        compiler_params=_params("parallel"),
        name="mm_ln",
    )(x, w, res, g, b)


def _ffn_kernel(x_ref, wg_ref, wu_ref, wd_ref, res_ref, g_ref, b_ref, of_ref, ob_ref, acc_ref, *, alpha):
    f = pl.program_id(1)

    @pl.when(f == 0)
    def _():
        acc_ref[...] = jnp.zeros_like(acc_ref)

    x = x_ref[...]
    gate = jnp.dot(x, wg_ref[...], preferred_element_type=F32)
    up = jnp.dot(x, wu_ref[...], preferred_element_type=F32)
    h = (gate * jax.nn.sigmoid(gate) * up).astype(BF16)
    acc_ref[...] += jnp.dot(h, wd_ref[...], preferred_element_type=F32)

    @pl.when(f == pl.num_programs(1) - 1)
    def _():
        y = _layer_norm(alpha * res_ref[...] + acc_ref[...], g_ref[...], b_ref[...])
        of_ref[...] = y
        ob_ref[...] = y.astype(BF16)


def _ffn(x, wg, wu, wd, layer, res, g, b, alpha, tm_pref=512, tf_pref=512):
    m, d = x.shape
    ff = wg.shape[2]
    tm, tf = _tile(m, tm_pref), _tile(ff, tf_pref)
    return pl.pallas_call(
        functools.partial(_ffn_kernel, alpha=alpha),
        grid=(m // tm, ff // tf),
        in_specs=[pl.BlockSpec((tm, d), lambda i, f: (i, 0)),
                  pl.BlockSpec((None, d, tf), lambda i, f: (layer, 0, f)),
                  pl.BlockSpec((None, d, tf), lambda i, f: (layer, 0, f)),
                  pl.BlockSpec((None, tf, d), lambda i, f: (layer, f, 0)),
                  pl.BlockSpec((tm, d), lambda i, f: (i, 0)),
                  pl.BlockSpec((1, d), lambda i, f: (0, 0)),
                  pl.BlockSpec((1, d), lambda i, f: (0, 0))],
        out_specs=[pl.BlockSpec((tm, d), lambda i, f: (i, 0)),
                   pl.BlockSpec((tm, d), lambda i, f: (i, 0))],
        out_shape=[jax.ShapeDtypeStruct((m, d), F32), jax.ShapeDtypeStruct((m, d), BF16)],
        scratch_shapes=[pltpu.VMEM((tm, d), F32)],
        compiler_params=_params("parallel", "arbitrary"),
        name="ffn",
    )(x, wg, wu, wd, res, g, b)


def _evenmix_kernel(qkv_ref, rest_ref, gate_ref, bg_ref, cw_ref,
                    mix_ref, c_out, n_out, m_out, cv_out,
                    c_s, n_s, m_s, u_s, *, nh, dk, dv, dc, chunk):
    c = pl.program_id(1)
    last = pl.num_programs(1) - 1
    dm = nh * dk
    L = chunk

    @pl.when(c == 0)
    def _():
        c_s[...] = jnp.zeros_like(c_s)
        n_s[...] = jnp.zeros_like(n_s)
        m_s[...] = jnp.zeros_like(m_s)
        u_s[...] = jnp.zeros_like(u_s)

    g = gate_ref[...] + bg_ref[...]
    lane = lax.broadcasted_iota(jnp.int32, (L, LANES), 1)
    row = lax.broadcasted_iota(jnp.int32, (L, LANES), 0)
    bcum = _log_sigmoid(g)
    s = 1
    while s < L:
        bcum = bcum + jnp.where(row >= s, pltpu.roll(bcum, s, 0), 0.0)
        s *= 2
    z = jnp.where(lane < nh, g, bcum)
    zt = z.T

    tr = lax.broadcasted_iota(jnp.int32, (L, L), 0)
    tc = lax.broadcasted_iota(jnp.int32, (L, L), 1)
    causal = tr >= tc
    lane1 = lax.broadcasted_iota(jnp.int32, (1, LANES), 1)
    m_vec = jnp.zeros((1, LANES), F32)

    for h in range(nh):
        qc = qkv_ref[:, h * dk:(h + 1) * dk]
        kc = qkv_ref[:, dm + h * dk:dm + (h + 1) * dk] * (dk ** -0.5)
        vc = qkv_ref[:, 2 * dm + h * dv:2 * dm + (h + 1) * dv]
        ig_col = z[:, h:h + 1]
        b_col = z[:, nh + h:nh + h + 1]
        ig_row = zt[h:h + 1, :]
        b_row = zt[nh + h:nh + h + 1, :]
        m_prev = m_s[h][:, 0:1]
        cmat = c_s[h]
        n_row = n_s[h]

        dmat = jnp.where(causal, b_col - b_row + ig_row, -jnp.inf)
        inter = b_col + m_prev
        m_t = jnp.maximum(inter, jnp.max(dmat, axis=-1, keepdims=True))
        w_intra = jnp.exp(dmat - m_t)
        w_inter = jnp.exp(inter - m_t)
        sc = lax.dot_general(qc, kc, (((1,), (1,)), ((), ())), preferred_element_type=F32) * w_intra
        num = (w_inter * jnp.dot(qc, cmat.astype(BF16), preferred_element_type=F32)
               + jnp.dot(sc.astype(BF16), vc, preferred_element_type=F32))
        qn = jnp.sum(qc.astype(F32) * n_row, axis=-1, keepdims=True)
        den = w_inter * qn + jnp.sum(sc, axis=-1, keepdims=True)
        out = num * (1.0 / jnp.maximum(jnp.abs(den), jnp.exp(-m_t)))

        m_new = m_t[L - 1:L, :]
        b_last = b_col[L - 1:L, :]
        g_inter = jnp.exp(b_last + m_prev - m_new)
        g_s = jnp.exp(b_last - b_col + ig_col - m_new)
        kg = kc.astype(F32) * g_s
        c_s[h] = g_inter * cmat + jnp.dot(kg.T.astype(BF16), vc, preferred_element_type=F32)
        n_s[h] = g_inter * n_row + jnp.sum(kg, axis=0, keepdims=True)
        m_s[h] = jnp.broadcast_to(m_new, (1, LANES))
        m_vec = jnp.where(lane1 == h, m_new, m_vec)

        o_gate = rest_ref[:, h * dv:(h + 1) * dv]
        mix_ref[:, h * dv:(h + 1) * dv] = (jax.nn.sigmoid(o_gate) * out).astype(BF16)

    base = nh * dv
    gb = rest_ref[:, base:base + dc]
    u = rest_ref[:, base + dc:base + 2 * dc] * rest_ref[:, base + 2 * dc:base + 3 * dc]
    prev2 = u_s[0:1, :]
    prev1 = u_s[1:2, :]
    r1, r2 = pltpu.roll(u, 1, 0), pltpu.roll(u, 2, 0)
    row8 = lax.broadcasted_iota(jnp.int32, (SUBLANES, dc), 0)
    u1 = jnp.concatenate([jnp.where(row8 == 0, prev1, r1[:SUBLANES]), r1[SUBLANES:]], axis=0)
    u2 = jnp.concatenate([jnp.where(row8 == 0, prev2, jnp.where(row8 == 1, prev1, r2[:SUBLANES])),
                          r2[SUBLANES:]], axis=0)
    yc = gb * (cw_ref[0:1, :] * u2 + cw_ref[1:2, :] * u1 + cw_ref[2:3, :] * u)
    mix_ref[:, dm:dm + dc] = yc.astype(BF16)
    u_s[0:1, :] = u[L - 2:L - 1, :]
    u_s[1:2, :] = u[L - 1:L, :]

    @pl.when(c == last)
    def _():
        c_out[0] = c_s[...]
        n_out[0] = n_s[...]
        m_out[0] = m_vec
        cv_out[0, 0:1, :] = u[L - 2:L - 1, :]
        cv_out[0, 1:2, :] = u[L - 1:L, :]


def _evenmix_prompt(qkv, rest, gates, b_gate_pad, conv_w, bsz, t, nh, dk, dv, dc, chunk):
    nc = t // chunk
    dm = nh * dk
    kern = functools.partial(_evenmix_kernel, nh=nh, dk=dk, dv=dv, dc=dc, chunk=chunk)
    return pl.pallas_call(
        kern,
        grid=(bsz, nc),
        in_specs=[pl.BlockSpec((chunk, 3 * dm), lambda b, c: (b * nc + c, 0)),
                  pl.BlockSpec((chunk, dm + 3 * dc), lambda b, c: (b * nc + c, 0)),
                  pl.BlockSpec((chunk, LANES), lambda b, c: (b * nc + c, 0)),
                  pl.BlockSpec((1, LANES), lambda b, c: (0, 0)),
                  pl.BlockSpec((CONV_W, dc), lambda b, c: (0, 0))],
        out_specs=[pl.BlockSpec((chunk, dm + dc), lambda b, c: (b * nc + c, 0)),
                   pl.BlockSpec((1, nh, dk, dv), lambda b, c: (b, 0, 0, 0)),
                   pl.BlockSpec((1, nh, 1, dk), lambda b, c: (b, 0, 0, 0)),
                   pl.BlockSpec((1, 1, LANES), lambda b, c: (b, 0, 0)),
                   pl.BlockSpec((1, CONV_W - 1, dc), lambda b, c: (b, 0, 0))],
        out_shape=[jax.ShapeDtypeStruct((bsz * t, dm + dc), BF16),
                   jax.ShapeDtypeStruct((bsz, nh, dk, dv), F32),
                   jax.ShapeDtypeStruct((bsz, nh, 1, dk), F32),
                   jax.ShapeDtypeStruct((bsz, 1, LANES), F32),
                   jax.ShapeDtypeStruct((bsz, CONV_W - 1, dc), F32)],
        scratch_shapes=[pltpu.VMEM((nh, dk, dv), F32),
                        pltpu.VMEM((nh, 1, dk), F32),
                        pltpu.VMEM((nh, 1, LANES), F32),
                        pltpu.VMEM((SUBLANES, dc), F32)],
        compiler_params=_params("parallel", "arbitrary"),
        name="evenmix_prompt",
    )(qkv, rest, gates, b_gate_pad, conv_w)


def _evenmix_step_kernel(q_ref, k_ref, v_ref, o_ref, conv_ref, gate_ref, bg_ref, cw_ref,
                         c_ref, n_ref, m_ref, cv_ref,
                         mix_ref, c_out, n_out, m_out, cv_out, *, nh, dk, dv, dc, nseq):
    dm = nh * dk
    scale = dk ** -0.5
    lane1 = lax.broadcasted_iota(jnp.int32, (1, LANES), 1)
    row128 = lax.broadcasted_iota(jnp.int32, (LANES, dk), 0)
    for b in range(nseq):
        q_rows = q_ref[b]
        k_rows = k_ref[b] * scale
        tile = jnp.zeros((LANES, dk), F32)
        for h in range(nh):
            tile = jnp.where(row128 == h, q_rows[h:h + 1, :], tile)
            tile = jnp.where(row128 == nh + h, k_rows[h:h + 1, :], tile)
        cols = tile.T
        g = gate_ref[b] + bg_ref[...]
        lf_all = _log_sigmoid(g)
        m_all = m_ref[b]
        m_vec = jnp.zeros((1, LANES), F32)
        for h in range(nh):
            q_col = cols[:, h:h + 1]
            k_col = cols[:, nh + h:nh + h + 1]
            q_row = q_rows[h:h + 1, :]
            k_row = k_rows[h:h + 1, :]
            v_row = v_ref[b, h:h + 1, :]
            ig = g[:, h:h + 1]
            lf = lf_all[:, nh + h:nh + h + 1]
            m_prev = m_all[:, h:h + 1]
            cmat = c_ref[b, h]
            n_row = n_ref[b, h]

            inter = lf + m_prev
            m_t = jnp.maximum(inter, ig)
            w_intra = jnp.exp(ig - m_t)
            w_inter = jnp.exp(inter - m_t)
            sc = jnp.sum(q_row * k_row, axis=-1, keepdims=True) * w_intra
            num = w_inter * jnp.sum(q_col * cmat, axis=0, keepdims=True) + sc * v_row
            den = w_inter * jnp.sum(q_row * n_row, axis=-1, keepdims=True) + sc
            out = num / jnp.maximum(jnp.abs(den), jnp.exp(-m_t))

            g_inter = jnp.exp(lf + m_prev - m_t)
            g_s = jnp.exp(ig - m_t)
            c_out[b, h] = g_inter * cmat + (g_s * k_col) * v_row
            n_out[b, h] = g_inter * n_row + g_s * k_row
            m_vec = jnp.where(lane1 == h, m_t, m_vec)
            mix_ref[b, :, h * dv:(h + 1) * dv] = jax.nn.sigmoid(o_ref[b, h:h + 1, :]) * out
        m_out[b] = m_vec

        gb = conv_ref[b, 0:1, :]
        u = conv_ref[b, 1:2, :] * conv_ref[b, 2:3, :]
        prev2 = cv_ref[b, 0:1, :]
        prev1 = cv_ref[b, 1:2, :]
        yc = gb * (cw_ref[0:1, :] * prev2 + cw_ref[1:2, :] * prev1 + cw_ref[2:3, :] * u)
        mix_ref[b, :, dm:dm + dc] = yc
        cv_out[b, 0:1, :] = prev1
        cv_out[b, 1:2, :] = u


def _evenmix_step(q, k, v, o, conv_in, gates, b_gate_pad, conv_w, c0, n0, m0_pad, cv0, nh, dk, dv, dc):
    bsz = q.shape[0]
    dm = nh * dk
    assert 2 * nh <= LANES
    nseq = _tile(bsz, 4)
    head_spec = lambda w: pl.BlockSpec((nseq, nh, w), lambda b: (b, 0, 0))
    kern = functools.partial(_evenmix_step_kernel, nh=nh, dk=dk, dv=dv, dc=dc, nseq=nseq)
    return pl.pallas_call(
        kern,
        grid=(bsz // nseq,),
        in_specs=[head_spec(dk), head_spec(dk), head_spec(dv), head_spec(dv),
                  pl.BlockSpec((nseq, 3, dc), lambda b: (b, 0, 0)),
                  pl.BlockSpec((nseq, 1, LANES), lambda b: (b, 0, 0)),
                  pl.BlockSpec((1, LANES), lambda b: (0, 0)),
                  pl.BlockSpec((CONV_W, dc), lambda b: (0, 0)),
                  pl.BlockSpec((nseq, nh, dk, dv), lambda b: (b, 0, 0, 0)),
                  pl.BlockSpec((nseq, nh, 1, dk), lambda b: (b, 0, 0, 0)),
                  pl.BlockSpec((nseq, 1, LANES), lambda b: (b, 0, 0)),
                  pl.BlockSpec((nseq, CONV_W - 1, dc), lambda b: (b, 0, 0))],
        out_specs=[pl.BlockSpec((nseq, 1, dm + dc), lambda b: (b, 0, 0)),
                   pl.BlockSpec((nseq, nh, dk, dv), lambda b: (b, 0, 0, 0)),
                   pl.BlockSpec((nseq, nh, 1, dk), lambda b: (b, 0, 0, 0)),
                   pl.BlockSpec((nseq, 1, LANES), lambda b: (b, 0, 0)),
                   pl.BlockSpec((nseq, CONV_W - 1, dc), lambda b: (b, 0, 0))],
        out_shape=[jax.ShapeDtypeStruct((bsz, 1, dm + dc), F32),
                   jax.ShapeDtypeStruct((bsz, nh, dk, dv), F32),
                   jax.ShapeDtypeStruct((bsz, nh, 1, dk), F32),
                   jax.ShapeDtypeStruct((bsz, 1, LANES), F32),
                   jax.ShapeDtypeStruct((bsz, CONV_W - 1, dc), F32)],
        compiler_params=_params("parallel"),
        name="evenmix_step",
    )(q.reshape(bsz, nh, dk), k.reshape(bsz, nh, dk), v.reshape(bsz, nh, dv), o.reshape(bsz, nh, dv),
      conv_in, gates, b_gate_pad, conv_w, c0, n0, m0_pad, cv0)


def _diff_lambda(lamv_ref, lam_init):
    a = jnp.sum(lamv_ref[0:1, :] * lamv_ref[1:2, :], axis=-1, keepdims=True)
    b = jnp.sum(lamv_ref[2:3, :] * lamv_ref[3:4, :], axis=-1, keepdims=True)
    return jnp.exp(a) - jnp.exp(b) + lam_init


def _attn_prompt_kernel(slope_ref, q_ref, k_ref, v_ref, lamv_ref, sg_ref, o_ref,
                        m_s, l_s, acc_s, kb_s, sa_s, sb_s, *, dh, dv, blk, lam_init):
    h = pl.program_id(1)
    qi = pl.program_id(2)
    t_all = k_ref.shape[0]
    nslab = blk // LANES
    nhalf = dv // LANES

    @pl.when(qi == 0)
    def _():
        pos = (lax.broadcasted_iota(jnp.int32, (t_all, LANES), 0).astype(F32)
               * (slope_ref[h] * LOG2E))
        lane = lax.broadcasted_iota(jnp.int32, (t_all, LANES), 1)
        hi = pos.astype(BF16).astype(F32)
        r1 = pos - hi
        mid = r1.astype(BF16).astype(F32)
        lo = r1 - mid
        pieces = jnp.where(lane == 0, hi, jnp.where(lane == 1, mid, jnp.where(lane == 2, lo, 0.0)))
        kb_s[...] = pieces.astype(BF16)

    m_s[...] = jnp.full_like(m_s, -jnp.inf)
    l_s[...] = jnp.zeros_like(l_s)
    acc_s[...] = jnp.zeros_like(acc_s)

    def scores(ki, s_ref):
        start = pl.multiple_of(ki * blk, blk)
        lane_q = lax.broadcasted_iota(jnp.int32, (blk, LANES), 1)
        ones_cols = jnp.where(lane_q < 3, 1.0, 0.0).astype(BF16)
        kb = kb_s[pl.ds(start, blk), :]
        for c in range(2):
            qa = jnp.concatenate([q_ref[:, c * dh:(c + 1) * dh], ones_cols], axis=1)
            ka = jnp.concatenate([k_ref[pl.ds(start, blk), c * dh:(c + 1) * dh], kb], axis=1)
            s_ref[c] = lax.dot_general(qa, ka, (((1,), (1,)), ((), ())), preferred_element_type=F32)

    def accumulate(ki, s_ref, masked):
        start = pl.multiple_of(ki * blk, blk)
        v = v_ref[pl.ds(start, blk), :]
        for c in range(2):
            s = s_ref[c]
            if masked:
                row = lax.broadcasted_iota(jnp.int32, (blk, blk), 0)
                col = lax.broadcasted_iota(jnp.int32, (blk, blk), 1)
                s = jnp.where(row >= col, s, -jnp.inf)
            slabs = [s[:, j * LANES:(j + 1) * LANES] for j in range(nslab)]
            mx = functools.reduce(jnp.maximum, slabs)
            m_prev = m_s[c]
            m_new = jnp.maximum(m_prev, jnp.max(mx, axis=-1, keepdims=True))
            alpha = jnp.exp2(m_prev - m_new)
            ps = [jnp.exp2(sl - m_new) for sl in slabs]
            psum = functools.reduce(lambda a, b: a + b, ps)
            l_s[c] = alpha * l_s[c] + jnp.sum(psum, axis=-1, keepdims=True)
            m_s[c] = m_new
            pv = jnp.dot(jnp.concatenate([x.astype(BF16) for x in ps], axis=1), v,
                         preferred_element_type=F32)
            for e in range(nhalf):
                sl = slice(e * LANES, (e + 1) * LANES)
                acc_s[c, :, sl] = alpha * acc_s[c, :, sl] + pv[:, sl]

    npairs = qi // 2
    scores(0, sa_s)

    def body(j, carry):
        scores(2 * j + 1, sb_s)
        accumulate(2 * j, sa_s, masked=False)
        scores(2 * j + 2, sa_s)
        accumulate(2 * j + 1, sb_s, masked=False)
        return carry

    lax.fori_loop(0, npairs, body, 0)

    @pl.when(qi == 2 * npairs)
    def _():
        accumulate(qi, sa_s, masked=True)

    @pl.when(qi != 2 * npairs)
    def _():
        scores(qi, sb_s)
        accumulate(qi - 1, sa_s, masked=False)
        accumulate(qi, sb_s, masked=True)

    lam = _diff_lambda(lamv_ref, lam_init)
    inv0 = 1.0 / l_s[0]
    inv1 = lam / l_s[1]
    halves = [acc_s[0, :, e * LANES:(e + 1) * LANES] * inv0 - acc_s[1, :, e * LANES:(e + 1) * LANES] * inv1
              for e in range(nhalf)]
    ssq = functools.reduce(lambda a, b: a + b, [jnp.sum(o * o, axis=-1, keepdims=True) for o in halves])
    inv = lax.rsqrt(ssq / dv + LN_EPS) * (1.0 - lam_init)
    for e in range(nhalf):
        sl = slice(e * LANES, (e + 1) * LANES)
        o_ref[:, sl] = (halves[e] * inv * sg_ref[:, sl]).astype(BF16)


def _attn_prompt(q, k, v, slopes, lamv, subln_g, bsz, t, nh, dh, dv, lam_init, blk_pref=512):
    blk = _tile(t, blk_pref)
    nq = t // blk
    kern = functools.partial(_attn_prompt_kernel, dh=dh, dv=dv, blk=blk, lam_init=lam_init)
    return pl.pallas_call(
        kern,
        grid=(bsz, nh, nq),
        in_specs=[pl.BlockSpec(memory_space=pltpu.SMEM),
                  pl.BlockSpec((blk, 2 * dh), lambda b, h, qi: (b * nq + qi, h)),
                  pl.BlockSpec((t, 2 * dh), lambda b, h, qi: (b, h)),
                  pl.BlockSpec((t, dv), lambda b, h, qi: (b, h)),
                  pl.BlockSpec((4, dh), lambda b, h, qi: (0, 0)),
                  pl.BlockSpec((1, dv), lambda b, h, qi: (0, 0))],
        out_specs=pl.BlockSpec((blk, dv), lambda b, h, qi: (b * nq + qi, h)),
        out_shape=jax.ShapeDtypeStruct((bsz * t, nh * dv), BF16),
        scratch_shapes=[pltpu.VMEM((2, blk, LANES), F32),
                        pltpu.VMEM((2, blk, LANES), F32),
                        pltpu.VMEM((2, blk, dv), F32),
                        pltpu.VMEM((t, LANES), BF16),
                        pltpu.VMEM((2, blk, blk), F32),
                        pltpu.VMEM((2, blk, blk), F32)],
        compiler_params=_params("parallel", "parallel", "arbitrary"),
        name="attn_prompt",
    )(slopes, q, k, v, lamv, subln_g)


def _attn_step_kernel(pt_ref, q_ref, kn_ref, vn_ref, slope_ref, lamv_ref, sg_ref, *refs,
                      dh, dv, page, pages_per_step, past, lam_init):
    del pt_ref
    npg = pages_per_step
    k_refs = refs[:npg]
    v_refs = refs[npg:2 * npg]
    o_ref = refs[2 * npg]
    m_s, l_s, acc_s, pos_s, q_s = refs[2 * npg + 1:]
    pp = pl.program_id(1)
    nh = q_ref.shape[1]
    nr, nl = 2 * nh, page * nh
    slope2 = jnp.concatenate([slope_ref[...], slope_ref[...]], axis=0)[:, 0:1]

    @pl.when(pp == 0)
    def _():
        q, kn, vn = q_ref[0], kn_ref[0], vn_ref[0]
        zeros = jnp.zeros((nh, dh), F32)
        q_s[...] = jnp.concatenate([jnp.concatenate([q[:, :dh], zeros], axis=1),
                                    jnp.concatenate([zeros, q[:, dh:]], axis=1)], axis=0)
        lane = lax.broadcasted_iota(jnp.int32, (nr, nl), 1)
        row = lax.broadcasted_iota(jnp.int32, (nr, nl), 0)
        same_head = jnp.bitwise_and(lane, nh - 1) == jnp.bitwise_and(row, nh - 1)
        token = lax.shift_right_logical(lane, nh.bit_length() - 1).astype(F32)
        pos_s[...] = jnp.where(same_head, slope2 * token, -jnp.inf)
        prod = q * kn
        m_s[...] = jnp.concatenate([jnp.sum(prod[:, :dh], axis=-1, keepdims=True),
                                    jnp.sum(prod[:, dh:], axis=-1, keepdims=True)], axis=0)
        l_s[...] = jnp.ones_like(l_s)
        acc_s[...] = jnp.concatenate([vn, vn], axis=0)

    qb = q_s[...]
    scores = []
    mx = None
    for i in range(npg):
        first_pos = (pp * npg + i) * page - past
        s = lax.dot_general(qb, k_refs[i][...].reshape(nl, 2 * dh), (((1,), (1,)), ((), ())),
                            preferred_element_type=F32)
        s = s + pos_s[...] + slope2 * first_pos.astype(F32)
        scores.append(s)
        cur = jnp.max(s, axis=-1, keepdims=True)
        mx = cur if mx is None else jnp.maximum(mx, cur)
    m_prev = m_s[...]
    m_new = jnp.maximum(m_prev, mx)
    alpha = jnp.exp2(m_prev - m_new)
    lsum = jnp.zeros_like(m_new)
    pv = jnp.zeros((nr, dv), F32)
    for i in range(npg):
        p = jnp.exp2(scores[i] - m_new)
        lsum = lsum + jnp.sum(p, axis=-1, keepdims=True)
        pv = pv + jnp.dot(p, v_refs[i][...].reshape(nl, dv), preferred_element_type=F32)
    l_s[...] = alpha * l_s[...] + lsum
    acc_s[...] = alpha * acc_s[...] + pv
    m_s[...] = m_new

    @pl.when(pp == pl.num_programs(1) - 1)
    def _():
        lam = _diff_lambda(lamv_ref, lam_init)
        a, l = acc_s[...], l_s[...]
        o = a[:nh] / l[:nh] - lam * (a[nh:] / l[nh:])
        o = o * lax.rsqrt(jnp.mean(o * o, axis=-1, keepdims=True) + LN_EPS) * sg_ref[...]
        o_ref[0] = o * (1.0 - lam_init)


def _attn_step(q, kn, vn, cache_k, cache_v, layer_j, page_table, slopes_rep, lamv, subln_g,
               nh, dh, dv, lam_init, pages_per_step=8):
    bsz, n_pages = page_table.shape
    page = cache_k.shape[2]
    past = n_pages * page
    assert nh & (nh - 1) == 0, "the (token, head) lane split uses bit masks"
    pps = pages_per_step if n_pages % pages_per_step == 0 else 1
    kern = functools.partial(_attn_step_kernel, dh=dh, dv=dv, page=page, pages_per_step=pps,
                             past=past, lam_init=lam_init)

    def page_spec(i, width):
        return pl.BlockSpec((None, None, page, nh, width),
                            lambda b, pp, pt: (layer_j, pt[b * n_pages + pp * pps + i], 0, 0, 0))

    tok_spec = lambda width: pl.BlockSpec((1, nh, width), lambda b, pp, pt: (b, 0, 0))
    grid_spec = pltpu.PrefetchScalarGridSpec(
        num_scalar_prefetch=1,
        grid=(bsz, n_pages // pps),
        in_specs=[tok_spec(2 * dh), tok_spec(2 * dh), tok_spec(dv),
                  pl.BlockSpec((nh, LANES), lambda b, pp, pt: (0, 0)),
                  pl.BlockSpec((4, dh), lambda b, pp, pt: (0, 0)),
                  pl.BlockSpec((1, dv), lambda b, pp, pt: (0, 0))]
                 + [page_spec(i, 2 * dh) for i in range(pps)]
                 + [page_spec(i, dv) for i in range(pps)],
        out_specs=tok_spec(dv),
        scratch_shapes=[pltpu.VMEM((2 * nh, 1), F32),
                        pltpu.VMEM((2 * nh, 1), F32),
                        pltpu.VMEM((2 * nh, dv), F32),
                        pltpu.VMEM((2 * nh, page * nh), F32),
                        pltpu.VMEM((2 * nh, 2 * dh), F32)],
    )
    return pl.pallas_call(
        kern,
        grid_spec=grid_spec,
        out_shape=jax.ShapeDtypeStruct((bsz, nh, dv), F32),
        compiler_params=_params("parallel", "arbitrary"),
        name="attn_step",
    )(page_table.reshape(-1), q, kn, vn, slopes_rep, lamv, subln_g,
      *([cache_k] * pps), *([cache_v] * pps))


def kernel(x_prompt, x_sample, state_mlstm_c, state_mlstm_n, state_mlstm_m, state_conv, cache_k, cache_v, page_table, w_in_even, b_gate_even, conv_w_even, w_out_even, w_qkv_odd, w_o_odd, lambda_q1, lambda_k1, lambda_q2, lambda_k2, subln_g, w_ffn_gate, w_ffn_up, w_ffn_down, ln_mix_g, ln_mix_b, ln_ffn_g, ln_ffn_b):
    bp, t, d = x_prompt.shape
    bs = x_sample.shape[0]
    assert x_sample.shape[1] == 1, "the sample group decodes one token per sequence"
    depth = w_ffn_gate.shape[0]
    nh_a, dk_a, dv_a = state_mlstm_c.shape[2:]
    dm = nh_a * dk_a
    dc = state_conv.shape[-1]
    nh_c, dv_c = cache_v.shape[3:]
    dh_c = cache_k.shape[4] // 2
    alpha = (2 * depth) ** 0.25
    chunk = math.gcd(t, 256)

    xp = x_prompt.reshape(bp * t, d)
    xs = x_sample.reshape(bs, d)
    xp_b = xs_b = None
    slopes = jnp.exp2(-8.0 * jnp.arange(1, nh_c + 1, dtype=F32) / nh_c)
    slopes_rep = jnp.broadcast_to((slopes * LOG2E)[:, None], (nh_c, LANES))

    wg_all, wu_all, wd_all = w_ffn_gate.astype(BF16), w_ffn_up.astype(BF16), w_ffn_down.astype(BF16)

    pc, pn, pm, pconv, pk, pv = [], [], [], [], [], []
    sc, sn, sm, sconv, sk, sv = [], [], [], [], [], []
    for layer in range(depth):
        j = layer // 2
        g_mix, b_mix = ln_mix_g[layer][None], ln_mix_b[layer][None]
        g_ffn, b_ffn = ln_ffn_g[layer][None], ln_ffn_b[layer][None]
        if layer % 2 == 0:
            w_in = w_in_even[j]
            w_qkv = w_in[:, :3 * dm].astype(BF16)
            gate_lo = 4 * dm
            w_gate = jnp.pad(w_in[:, gate_lo:gate_lo + 2 * nh_a], ((0, 0), (0, LANES - 2 * nh_a))).astype(BF16)
            w_rest = jnp.concatenate([w_in[:, 3 * dm:4 * dm], w_in[:, gate_lo + 2 * nh_a:]], axis=1).astype(BF16)
            w_out = w_out_even[j].astype(BF16)
            b_gate_pad = jnp.pad(b_gate_even[j], (0, LANES - 2 * nh_a))[None]
            conv_w = conv_w_even[j]

            x_in = xp if xp_b is None else xp_b
            (qkv,) = _mm(x_in, w_qkv, [BF16])
            (rest,) = _mm(x_in, w_rest, [F32])
            (gates,) = _mm(x_in, w_gate, [F32])
            mix, c1, n1, m1, cv1 = _evenmix_prompt(qkv, rest, gates, b_gate_pad, conv_w,
                                                   bp, t, nh_a, dk_a, dv_a, dc, chunk)
            xp, xp_b = _mm_ln(mix, w_out, xp, g_mix, b_mix, alpha)
            pc.append(c1); pn.append(n1.reshape(bp, nh_a, dk_a)); pm.append(m1[:, 0, :nh_a]); pconv.append(cv1)

            x_in = xs if xs_b is None else xs_b
            (qkv,) = _mm(x_in, w_qkv, [F32])
            (rest,) = _mm(x_in, w_rest, [F32])
            (gates,) = _mm(x_in, w_gate, [F32])
            conv_in = rest[:, dm:].reshape(bs, 3, dc)
            m0_pad = jnp.pad(state_mlstm_m[j], ((0, 0), (0, LANES - nh_a)))[:, None, :]
            mix, c2, n2, m2, cv2 = _evenmix_step(
                qkv[:, :dm], qkv[:, dm:2 * dm], qkv[:, 2 * dm:], rest[:, :dm], conv_in,
                gates[:, None, :], b_gate_pad, conv_w, state_mlstm_c[j],
                state_mlstm_n[j][:, :, None, :], m0_pad, state_conv[j], nh_a, dk_a, dv_a, dc)
            xs, xs_b = _mm_ln(mix.reshape(bs, dm + dc).astype(BF16), w_out, xs, g_mix, b_mix, alpha)
            sc.append(c2); sn.append(n2.reshape(bs, nh_a, dk_a)); sm.append(m2[:, 0, :nh_a]); sconv.append(cv2)
        else:
            lam_init = 0.8 - 0.6 * math.exp(-0.3 * layer)
            d_attn = nh_c * dv_c
            w_qkv = w_qkv_odd[j].astype(BF16)
            w_o = w_o_odd[j].astype(BF16)
            lamv = jnp.stack([lambda_q1[j], lambda_k1[j], lambda_q2[j], lambda_k2[j]])
            sg = subln_g[j][None]
            q_scale = dh_c ** -0.5 * LOG2E

            x_in = xp if xp_b is None else xp_b
            (q_b,) = _mm(x_in, w_qkv, [BF16], scale=q_scale, col0=0, n=d_attn, tn_pref=2048)
            k_f, k_b = _mm(x_in, w_qkv, [F32, BF16], col0=d_attn, n=d_attn, tn_pref=2048)
            v_f, v_b = _mm(x_in, w_qkv, [F32, BF16], col0=2 * d_attn, n=d_attn, tn_pref=2048)
            att = _attn_prompt(q_b, k_b, v_b, slopes, lamv, sg, bp, t, nh_c, dh_c, dv_c, lam_init)
            xp, xp_b = _mm_ln(att, w_o, xp, g_mix, b_mix, alpha)
            pk.append(k_f.reshape(bp, t, nh_c, 2 * dh_c)); pv.append(v_f.reshape(bp, t, nh_c, dv_c))

            x_in = xs if xs_b is None else xs_b
            (q_s,) = _mm(x_in, w_qkv, [F32], scale=q_scale, col0=0, n=d_attn)
            (k_s,) = _mm(x_in, w_qkv, [F32], col0=d_attn, n=d_attn)
            (v_s,) = _mm(x_in, w_qkv, [F32], col0=2 * d_attn, n=d_attn)
            att = _attn_step(q_s.reshape(bs, nh_c, 2 * dh_c), k_s.reshape(bs, nh_c, 2 * dh_c),
                             v_s.reshape(bs, nh_c, dv_c), cache_k, cache_v, j, page_table,
                             slopes_rep, lamv, sg, nh_c, dh_c, dv_c, lam_init)
            xs, xs_b = _mm_ln(att.reshape(bs, d_attn).astype(BF16), w_o, xs, g_mix, b_mix, alpha)
            sk.append(k_s.reshape(bs, 1, nh_c, 2 * dh_c)); sv.append(v_s.reshape(bs, 1, nh_c, dv_c))

        xp, xp_b = _ffn(xp_b, wg_all, wu_all, wd_all, layer, xp, g_ffn, b_ffn, alpha)
        xs, xs_b = _ffn(xs_b, wg_all, wu_all, wd_all, layer, xs, g_ffn, b_ffn, alpha)

    return (xp.reshape(bp, t, d), xs.reshape(bs, 1, d),
            jnp.stack(pc), jnp.stack(pn), jnp.stack(pm), jnp.stack(pconv), jnp.stack(pk), jnp.stack(pv),
            jnp.stack(sc), jnp.stack(sn), jnp.stack(sm), jnp.stack(sconv), jnp.stack(sk), jnp.stack(sv))
```

```python
import functools
import math

import jax
import jax.numpy as jnp
from jax import lax
from jax.experimental import pallas as pl
from jax.experimental.pallas import tpu as pltpu

F32 = jnp.float32
BF16 = jnp.bfloat16

LN_EPS = 1e-5
CONV_W = 3
LOG2E = math.log2(math.e)
V7X_VMEM_LIMIT_BYTES = 56 * 1024 * 1024
LANES = 128
SUBLANES = 8


def _params(*sem):
    return pltpu.CompilerParams(dimension_semantics=sem, vmem_limit_bytes=V7X_VMEM_LIMIT_BYTES)


def _layer_norm(y, g, b):
    mu = jnp.mean(y, axis=-1, keepdims=True)
    d = y - mu
    var = jnp.mean(d * d, axis=-1, keepdims=True)
    return d * lax.rsqrt(var + LN_EPS) * g + b


def _log_sigmoid(x):
    return jnp.minimum(x, 0.0) - jnp.log1p(jnp.exp(-jnp.abs(x)))


def _tile(n, pref):
    if n <= pref:
        return n
    t = pref
    while n % t:
        t //= 2
    return t


def _mm_kernel(x_ref, w_ref, *refs, scale, cast_x, n_out, has_side):
    refs = list(refs)
    ws_ref = refs.pop(0) if has_side else None
    out_refs = [refs.pop(0) for _ in range(n_out)]
    side_ref = refs.pop(0) if has_side else None
    first = pl.program_id(1) == 0
    if cast_x:
        xb_ref = refs.pop(0)

        @pl.when(first)
        def _():
            xb_ref[...] = x_ref[...].astype(BF16)

        x = xb_ref[...]
    else:
        x = x_ref[...]
    if has_side:
        @pl.when(first)
        def _():
            side_ref[...] = jnp.dot(x, ws_ref[...], preferred_element_type=F32)

    acc = jnp.dot(x, w_ref[...], preferred_element_type=F32)
    if scale is not None:
        acc = acc * scale
    for o_ref in out_refs:
        o_ref[...] = acc.astype(o_ref.dtype)


def _mm(x, w, out_dtypes, scale=None, col0=0, n=None, w_side=None, tm_pref=1024, tn_pref=1024):
    m, k = x.shape
    n = w.shape[1] if n is None else n
    tm, tn = _tile(m, tm_pref), _tile(n, tn_pref)
    assert col0 % tn == 0
    j0 = col0 // tn
    cast_x = x.dtype != BF16
    has_side = w_side is not None
    kern = functools.partial(_mm_kernel, scale=scale, cast_x=cast_x, n_out=len(out_dtypes), has_side=has_side)
    in_specs = [pl.BlockSpec((tm, k), lambda i, j: (i, 0)),
                pl.BlockSpec((k, tn), lambda i, j: (0, j + j0))]
    out_specs = [pl.BlockSpec((tm, tn), lambda i, j: (i, j)) for _ in out_dtypes]
    out_shape = [jax.ShapeDtypeStruct((m, n), dt) for dt in out_dtypes]
    operands = [x, w]
    if has_side:
        n_side = w_side.shape[1]
        in_specs.append(pl.BlockSpec((k, n_side), lambda i, j: (0, 0)))
        out_specs.append(pl.BlockSpec((tm, n_side), lambda i, j: (i, 0)))
        out_shape.append(jax.ShapeDtypeStruct((m, n_side), F32))
        operands.append(w_side)
    outs = pl.pallas_call(
        kern,
        grid=(m // tm, n // tn),
        in_specs=in_specs,
        out_specs=out_specs,
        out_shape=out_shape,
        scratch_shapes=[pltpu.VMEM((tm, k), BF16)] if cast_x else [],
        compiler_params=_params("parallel", "arbitrary"),
        name="mm",
    )(*operands)
    return outs


def _mm_ln_kernel(x_ref, w_ref, res_ref, g_ref, b_ref, of_ref, ob_ref, *, alpha, nsub):
    rows = x_ref.shape[0] // nsub
    for r in range(nsub):
        sl = slice(r * rows, (r + 1) * rows)
        f = jnp.dot(x_ref[sl, :], w_ref[...], preferred_element_type=F32)
        y = _layer_norm(alpha * res_ref[sl, :] + f, g_ref[...], b_ref[...])
        of_ref[sl, :] = y
        ob_ref[sl, :] = y.astype(BF16)


def _mm_ln(x, w, res, g, b, alpha, tm_pref=512):
    m, k = x.shape
    d = w.shape[1]
    tm = _tile(m, tm_pref)
    return pl.pallas_call(
        functools.partial(_mm_ln_kernel, alpha=alpha, nsub=2 if tm % 512 == 0 else 1),
        grid=(m // tm,),
        in_specs=[pl.BlockSpec((tm, k), lambda i: (i, 0)),
                  pl.BlockSpec((k, d), lambda i: (0, 0)),
                  pl.BlockSpec((tm, d), lambda i: (i, 0)),
                  pl.BlockSpec((1, d), lambda i: (0, 0)),
                  pl.BlockSpec((1, d), lambda i: (0, 0))],
        out_specs=[pl.BlockSpec((tm, d), lambda i: (i, 0)),
                   pl.BlockSpec((tm, d), lambda i: (i, 0))],
        out_shape=[jax.ShapeDtypeStruct((m, d), F32), jax.ShapeDtypeStruct((m, d), BF16)],
        compiler_params=_params("parallel"),
        name="mm_ln",
    )(x, w, res, g, b)


def _ffn_kernel(x_ref, wg_ref, wu_ref, wd_ref, res_ref, g_ref, b_ref, of_ref, ob_ref, acc_ref, *, alpha):
    f = pl.program_id(1)

    @pl.when(f == 0)
    def _():
        acc_ref[...] = jnp.zeros_like(acc_ref)

    x = x_ref[...]
    gate = jnp.dot(x, wg_ref[...], preferred_element_type=F32)
    up = jnp.dot(x, wu_ref[...], preferred_element_type=F32)
    h = (gate * jax.nn.sigmoid(gate) * up).astype(BF16)
    acc_ref[...] += jnp.dot(h, wd_ref[...], preferred_element_type=F32)

    @pl.when(f == pl.num_programs(1) - 1)
    def _():
        y = _layer_norm(alpha * res_ref[...] + acc_ref[...], g_ref[...], b_ref[...])
        of_ref[...] = y
        ob_ref[...] = y.astype(BF16)


def _ffn(x, wg, wu, wd, layer, res, g, b, alpha, tm_pref=512, tf_pref=512):
    m, d = x.shape
    ff = wg.shape[2]
    tm, tf = _tile(m, tm_pref), _tile(ff, tf_pref)
    return pl.pallas_call(
        functools.partial(_ffn_kernel, alpha=alpha),
        grid=(m // tm, ff // tf),
        in_specs=[pl.BlockSpec((tm, d), lambda i, f: (i, 0)),
                  pl.BlockSpec((None, d, tf), lambda i, f: (layer, 0, f)),
                  pl.BlockSpec((None, d, tf), lambda i, f: (layer, 0, f)),
                  pl.BlockSpec((None, tf, d), lambda i, f: (layer, f, 0)),
                  pl.BlockSpec((tm, d), lambda i, f: (i, 0)),
                  pl.BlockSpec((1, d), lambda i, f: (0, 0)),
                  pl.BlockSpec((1, d), lambda i, f: (0, 0))],
        out_specs=[pl.BlockSpec((tm, d), lambda i, f: (i, 0)),
                   pl.BlockSpec((tm, d), lambda i, f: (i, 0))],
        out_shape=[jax.ShapeDtypeStruct((m, d), F32), jax.ShapeDtypeStruct((m, d), BF16)],
        scratch_shapes=[pltpu.VMEM((tm, d), F32)],
        compiler_params=_params("parallel", "arbitrary"),
        name="ffn",
    )(x, wg, wu, wd, res, g, b)


def _evenmix_kernel(qkv_ref, rest_ref, gate_ref, bg_ref, cw_ref,
                    mix_ref, c_out, n_out, m_out, cv_out,
                    c_s, n_s, m_s, u_s, *, nh, dk, dv, dc, chunk):
    c = pl.program_id(1)
    last = pl.num_programs(1) - 1
    dm = nh * dk
    L = chunk

    @pl.when(c == 0)
    def _():
        c_s[...] = jnp.zeros_like(c_s)
        n_s[...] = jnp.zeros_like(n_s)
        m_s[...] = jnp.zeros_like(m_s)
        u_s[...] = jnp.zeros_like(u_s)

    g = gate_ref[...] + bg_ref[...]
    lane = lax.broadcasted_iota(jnp.int32, (L, LANES), 1)
    row = lax.broadcasted_iota(jnp.int32, (L, LANES), 0)
    bcum = _log_sigmoid(g)
    s = 1
    while s < L:
        bcum = bcum + jnp.where(row >= s, pltpu.roll(bcum, s, 0), 0.0)
        s *= 2
    z = jnp.where(lane < nh, g, bcum)
    zt = z.T

    tr = lax.broadcasted_iota(jnp.int32, (L, L), 0)
    tc = lax.broadcasted_iota(jnp.int32, (L, L), 1)
    causal = tr >= tc
    lane1 = lax.broadcasted_iota(jnp.int32, (1, LANES), 1)
    m_vec = jnp.zeros((1, LANES), F32)

    for h in range(nh):
        qc = qkv_ref[:, h * dk:(h + 1) * dk]
        kc = qkv_ref[:, dm + h * dk:dm + (h + 1) * dk] * (dk ** -0.5)
        vc = qkv_ref[:, 2 * dm + h * dv:2 * dm + (h + 1) * dv]
        ig_col = z[:, h:h + 1]
        b_col = z[:, nh + h:nh + h + 1]
        ig_row = zt[h:h + 1, :]
        b_row = zt[nh + h:nh + h + 1, :]
        m_prev = m_s[h][:, 0:1]
        cmat = c_s[h]
        n_row = n_s[h]

        dmat = jnp.where(causal, b_col - b_row + ig_row, -jnp.inf)
        inter = b_col + m_prev
        m_t = jnp.maximum(inter, jnp.max(dmat, axis=-1, keepdims=True))
        w_intra = jnp.exp(dmat - m_t)
        w_inter = jnp.exp(inter - m_t)
        sc = lax.dot_general(qc, kc, (((1,), (1,)), ((), ())), preferred_element_type=F32) * w_intra
        num = (w_inter * jnp.dot(qc, cmat.astype(BF16), preferred_element_type=F32)
               + jnp.dot(sc.astype(BF16), vc, preferred_element_type=F32))
        qn = jnp.sum(qc.astype(F32) * n_row, axis=-1, keepdims=True)
        den = w_inter * qn + jnp.sum(sc, axis=-1, keepdims=True)
        out = num * (1.0 / jnp.maximum(jnp.abs(den), jnp.exp(-m_t)))

        m_new = m_t[L - 1:L, :]
        b_last = b_col[L - 1:L, :]
        g_inter = jnp.exp(b_last + m_prev - m_new)
        g_s = jnp.exp(b_last - b_col + ig_col - m_new)
        kg = kc.astype(F32) * g_s
        c_s[h] = g_inter * cmat + jnp.dot(kg.T.astype(BF16), vc, preferred_element_type=F32)
        n_s[h] = g_inter * n_row + jnp.sum(kg, axis=0, keepdims=True)
        m_s[h] = jnp.broadcast_to(m_new, (1, LANES))
        m_vec = jnp.where(lane1 == h, m_new, m_vec)

        o_gate = rest_ref[:, h * dv:(h + 1) * dv]
        mix_ref[:, h * dv:(h + 1) * dv] = (jax.nn.sigmoid(o_gate) * out).astype(BF16)

    base = nh * dv
    gb = rest_ref[:, base:base + dc]
    u = rest_ref[:, base + dc:base + 2 * dc] * rest_ref[:, base + 2 * dc:base + 3 * dc]
    prev2 = u_s[0:1, :]
    prev1 = u_s[1:2, :]
    r1, r2 = pltpu.roll(u, 1, 0), pltpu.roll(u, 2, 0)
    row8 = lax.broadcasted_iota(jnp.int32, (SUBLANES, dc), 0)
    u1 = jnp.concatenate([jnp.where(row8 == 0, prev1, r1[:SUBLANES]), r1[SUBLANES:]], axis=0)
    u2 = jnp.concatenate([jnp.where(row8 == 0, prev2, jnp.where(row8 == 1, prev1, r2[:SUBLANES])),
                          r2[SUBLANES:]], axis=0)
    yc = gb * (cw_ref[0:1, :] * u2 + cw_ref[1:2, :] * u1 + cw_ref[2:3, :] * u)
    mix_ref[:, dm:dm + dc] = yc.astype(BF16)
    u_s[0:1, :] = u[L - 2:L - 1, :]
    u_s[1:2, :] = u[L - 1:L, :]

    @pl.when(c == last)
    def _():
        c_out[0] = c_s[...]
        n_out[0] = n_s[...]
        m_out[0] = m_vec
        cv_out[0, 0:1, :] = u[L - 2:L - 1, :]
        cv_out[0, 1:2, :] = u[L - 1:L, :]


def _evenmix_prompt(qkv, rest, gates, b_gate_pad, conv_w, bsz, t, nh, dk, dv, dc, chunk):
    nc = t // chunk
    dm = nh * dk
    kern = functools.partial(_evenmix_kernel, nh=nh, dk=dk, dv=dv, dc=dc, chunk=chunk)
    return pl.pallas_call(
        kern,
        grid=(bsz, nc),
        in_specs=[pl.BlockSpec((chunk, 3 * dm), lambda b, c: (b * nc + c, 0)),
                  pl.BlockSpec((chunk, dm + 3 * dc), lambda b, c: (b * nc + c, 0)),
                  pl.BlockSpec((chunk, LANES), lambda b, c: (b * nc + c, 0)),
                  pl.BlockSpec((1, LANES), lambda b, c: (0, 0)),
                  pl.BlockSpec((CONV_W, dc), lambda b, c: (0, 0))],
        out_specs=[pl.BlockSpec((chunk, dm + dc), lambda b, c: (b * nc + c, 0)),
                   pl.BlockSpec((1, nh, dk, dv), lambda b, c: (b, 0, 0, 0)),
                   pl.BlockSpec((1, nh, 1, dk), lambda b, c: (b, 0, 0, 0)),
                   pl.BlockSpec((1, 1, LANES), lambda b, c: (b, 0, 0)),
                   pl.BlockSpec((1, CONV_W - 1, dc), lambda b, c: (b, 0, 0))],
        out_shape=[jax.ShapeDtypeStruct((bsz * t, dm + dc), BF16),
                   jax.ShapeDtypeStruct((bsz, nh, dk, dv), F32),
                   jax.ShapeDtypeStruct((bsz, nh, 1, dk), F32),
                   jax.ShapeDtypeStruct((bsz, 1, LANES), F32),
                   jax.ShapeDtypeStruct((bsz, CONV_W - 1, dc), F32)],
        scratch_shapes=[pltpu.VMEM((nh, dk, dv), F32),
                        pltpu.VMEM((nh, 1, dk), F32),
                        pltpu.VMEM((nh, 1, LANES), F32),
                        pltpu.VMEM((SUBLANES, dc), F32)],
        compiler_params=_params("parallel", "arbitrary"),
        name="evenmix_prompt",
    )(qkv, rest, gates, b_gate_pad, conv_w)


def _evenmix_step_kernel(q_ref, k_ref, v_ref, o_ref, conv_ref, gate_ref, bg_ref, cw_ref,
                         c_ref, n_ref, m_ref, cv_ref,
                         mix_ref, c_out, n_out, m_out, cv_out, *, nh, dk, dv, dc, nseq):
    dm = nh * dk
    scale = dk ** -0.5
    lane1 = lax.broadcasted_iota(jnp.int32, (1, LANES), 1)
    row128 = lax.broadcasted_iota(jnp.int32, (LANES, dk), 0)
    for b in range(nseq):
        q_rows = q_ref[b]
        k_rows = k_ref[b] * scale
        tile = jnp.zeros((LANES, dk), F32)
        for h in range(nh):
            tile = jnp.where(row128 == h, q_rows[h:h + 1, :], tile)
            tile = jnp.where(row128 == nh + h, k_rows[h:h + 1, :], tile)
        cols = tile.T
        g = gate_ref[b] + bg_ref[...]
        lf_all = _log_sigmoid(g)
        m_all = m_ref[b]
        m_vec = jnp.zeros((1, LANES), F32)
        for h in range(nh):
            q_col = cols[:, h:h + 1]
            k_col = cols[:, nh + h:nh + h + 1]
            q_row = q_rows[h:h + 1, :]
            k_row = k_rows[h:h + 1, :]
            v_row = v_ref[b, h:h + 1, :]
            ig = g[:, h:h + 1]
            lf = lf_all[:, nh + h:nh + h + 1]
            m_prev = m_all[:, h:h + 1]
            cmat = c_ref[b, h]
            n_row = n_ref[b, h]

            inter = lf + m_prev
            m_t = jnp.maximum(inter, ig)
            w_intra = jnp.exp(ig - m_t)
            w_inter = jnp.exp(inter - m_t)
            sc = jnp.sum(q_row * k_row, axis=-1, keepdims=True) * w_intra
            num = w_inter * jnp.sum(q_col * cmat, axis=0, keepdims=True) + sc * v_row
            den = w_inter * jnp.sum(q_row * n_row, axis=-1, keepdims=True) + sc
            out = num / jnp.maximum(jnp.abs(den), jnp.exp(-m_t))

            g_inter = jnp.exp(lf + m_prev - m_t)
            g_s = jnp.exp(ig - m_t)
            c_out[b, h] = g_inter * cmat + (g_s * k_col) * v_row
            n_out[b, h] = g_inter * n_row + g_s * k_row
            m_vec = jnp.where(lane1 == h, m_t, m_vec)
            mix_ref[b, :, h * dv:(h + 1) * dv] = jax.nn.sigmoid(o_ref[b, h:h + 1, :]) * out
        m_out[b] = m_vec

        gb = conv_ref[b, 0:1, :]
        u = conv_ref[b, 1:2, :] * conv_ref[b, 2:3, :]
        prev2 = cv_ref[b, 0:1, :]
        prev1 = cv_ref[b, 1:2, :]
        yc = gb * (cw_ref[0:1, :] * prev2 + cw_ref[1:2, :] * prev1 + cw_ref[2:3, :] * u)
        mix_ref[b, :, dm:dm + dc] = yc
        cv_out[b, 0:1, :] = prev1
        cv_out[b, 1:2, :] = u


def _evenmix_step(q, k, v, o, conv_in, gates, b_gate_pad, conv_w, c0, n0, m0_pad, cv0, nh, dk, dv, dc):
    bsz = q.shape[0]
    dm = nh * dk
    assert 2 * nh <= LANES
    nseq = _tile(bsz, 4)
    head_spec = lambda w: pl.BlockSpec((nseq, nh, w), lambda b: (b, 0, 0))
    kern = functools.partial(_evenmix_step_kernel, nh=nh, dk=dk, dv=dv, dc=dc, nseq=nseq)
    return pl.pallas_call(
        kern,
        grid=(bsz // nseq,),
        in_specs=[head_spec(dk), head_spec(dk), head_spec(dv), head_spec(dv),
                  pl.BlockSpec((nseq, 3, dc), lambda b: (b, 0, 0)),
                  pl.BlockSpec((nseq, 1, LANES), lambda b: (b, 0, 0)),
                  pl.BlockSpec((1, LANES), lambda b: (0, 0)),
                  pl.BlockSpec((CONV_W, dc), lambda b: (0, 0)),
                  pl.BlockSpec((nseq, nh, dk, dv), lambda b: (b, 0, 0, 0)),
                  pl.BlockSpec((nseq, nh, 1, dk), lambda b: (b, 0, 0, 0)),
                  pl.BlockSpec((nseq, 1, LANES), lambda b: (b, 0, 0)),
                  pl.BlockSpec((nseq, CONV_W - 1, dc), lambda b: (b, 0, 0))],
        out_specs=[pl.BlockSpec((nseq, 1, dm + dc), lambda b: (b, 0, 0)),
                   pl.BlockSpec((nseq, nh, dk, dv), lambda b: (b, 0, 0, 0)),
                   pl.BlockSpec((nseq, nh, 1, dk), lambda b: (b, 0, 0, 0)),
                   pl.BlockSpec((nseq, 1, LANES), lambda b: (b, 0, 0)),
                   pl.BlockSpec((nseq, CONV_W - 1, dc), lambda b: (b, 0, 0))],
        out_shape=[jax.ShapeDtypeStruct((bsz, 1, dm + dc), F32),
                   jax.ShapeDtypeStruct((bsz, nh, dk, dv), F32),
                   jax.ShapeDtypeStruct((bsz, nh, 1, dk), F32),
                   jax.ShapeDtypeStruct((bsz, 1, LANES), F32),
                   jax.ShapeDtypeStruct((bsz, CONV_W - 1, dc), F32)],
        compiler_params=_params("parallel"),
        name="evenmix_step",
    )(q.reshape(bsz, nh, dk), k.reshape(bsz, nh, dk), v.reshape(bsz, nh, dv), o.reshape(bsz, nh, dv),
      conv_in, gates, b_gate_pad, conv_w, c0, n0, m0_pad, cv0)


def _diff_lambda(lamv_ref, lam_init):
    a = jnp.sum(lamv_ref[0:1, :] * lamv_ref[1:2, :], axis=-1, keepdims=True)
    b = jnp.sum(lamv_ref[2:3, :] * lamv_ref[3:4, :], axis=-1, keepdims=True)
    return jnp.exp(a) - jnp.exp(b) + lam_init


def _attn_prompt_kernel(slope_ref, q_ref, k_ref, v_ref, lamv_ref, sg_ref, o_ref, kb_s,
                        *, dh, dv, blk, lam_init):
    h = pl.program_id(1)
    t_all = k_ref.shape[0]
    nq = t_all // blk
    nslab = blk // LANES

    pos = (lax.broadcasted_iota(jnp.int32, (t_all, LANES), 0).astype(F32) * (slope_ref[h] * LOG2E))
    lane = lax.broadcasted_iota(jnp.int32, (t_all, LANES), 1)
    hi = pos.astype(BF16).astype(F32)
    r1 = pos - hi
    mid = r1.astype(BF16).astype(F32)
    lo = r1 - mid
    pieces = jnp.where(lane == 0, hi, jnp.where(lane == 1, mid, jnp.where(lane == 2, lo, 0.0)))
    kb_s[...] = pieces.astype(BF16)

    lane_q = lax.broadcasted_iota(jnp.int32, (blk, LANES), 1)
    ones_cols = jnp.where(lane_q < 3, 1.0, 0.0).astype(BF16)
    row = lax.broadcasted_iota(jnp.int32, (blk, blk), 0)
    col = lax.broadcasted_iota(jnp.int32, (blk, blk), 1)
    causal = row >= col
    lam = _diff_lambda(lamv_ref, lam_init)

    def scores(qi, ki):
        qs, ks = slice(qi * blk, (qi + 1) * blk), slice(ki * blk, (ki + 1) * blk)
        kb = kb_s[ks, :]
        out = []
        for c in range(2):
            qa = jnp.concatenate([q_ref[qs, c * dh:(c + 1) * dh], ones_cols], axis=1)
            ka = jnp.concatenate([k_ref[ks, c * dh:(c + 1) * dh], kb], axis=1)
            out.append(lax.dot_general(qa, ka, (((1,), (1,)), ((), ())), preferred_element_type=F32))
        return out

    for qi in range(nq):
        m, l, acc = [None, None], [None, None], [None, None]
        s_next = scores(qi, 0)
        for ki in range(qi + 1):
            s_cur = s_next
            if ki < qi:
                s_next = scores(qi, ki + 1)
            v = v_ref[ki * blk:(ki + 1) * blk, :]
            for c in range(2):
                s = s_cur[c]
                if ki == qi:
                    s = jnp.where(causal, s, -jnp.inf)
                slabs = [s[:, j * LANES:(j + 1) * LANES] for j in range(nslab)]
                m_cur = jnp.max(functools.reduce(jnp.maximum, slabs), axis=-1, keepdims=True)
                m_new = m_cur if ki == 0 else jnp.maximum(m[c], m_cur)
                ps = [jnp.exp2(sl - m_new) for sl in slabs]
                psum = jnp.sum(functools.reduce(lambda a, b: a + b, ps), axis=-1, keepdims=True)
                pv = jnp.dot(jnp.concatenate([x.astype(BF16) for x in ps], axis=1), v,
                             preferred_element_type=F32)
                if ki == 0:
                    l[c], acc[c] = psum, pv
                else:
                    alpha = jnp.exp2(m[c] - m_new)
                    l[c] = alpha * l[c] + psum
                    acc[c] = alpha * acc[c] + pv
                m[c] = m_new
        o = acc[0] * (1.0 / l[0]) - acc[1] * (lam / l[1])
        o = o * lax.rsqrt(jnp.mean(o * o, axis=-1, keepdims=True) + LN_EPS) * sg_ref[...]
        o_ref[qi * blk:(qi + 1) * blk, :] = (o * (1.0 - lam_init)).astype(BF16)


def _attn_prompt(q, k, v, slopes, lamv, subln_g, bsz, t, nh, dh, dv, lam_init, blk_pref=512):
    blk = _tile(t, blk_pref)
    kern = functools.partial(_attn_prompt_kernel, dh=dh, dv=dv, blk=blk, lam_init=lam_init)
    return pl.pallas_call(
        kern,
        grid=(bsz, nh),
        in_specs=[pl.BlockSpec(memory_space=pltpu.SMEM),
                  pl.BlockSpec((t, 2 * dh), lambda b, h: (b, h)),
                  pl.BlockSpec((t, 2 * dh), lambda b, h: (b, h)),
                  pl.BlockSpec((t, dv), lambda b, h: (b, h)),
                  pl.BlockSpec((4, dh), lambda b, h: (0, 0)),
                  pl.BlockSpec((1, dv), lambda b, h: (0, 0))],
        out_specs=pl.BlockSpec((t, dv), lambda b, h: (b, h)),
        out_shape=jax.ShapeDtypeStruct((bsz * t, nh * dv), BF16),
        scratch_shapes=[pltpu.VMEM((t, LANES), BF16)],
        compiler_params=_params("parallel", "parallel"),
        name="attn_prompt",
    )(slopes, q, k, v, lamv, subln_g)


def _attn_step_kernel(pt_ref, q_ref, kn_ref, vn_ref, slope_ref, lamv_ref, sg_ref, *refs,
                      dh, dv, page, pages_per_step, past, lam_init):
    del pt_ref
    npg = pages_per_step
    k_refs = refs[:npg]
    v_refs = refs[npg:2 * npg]
    o_ref = refs[2 * npg]
    m_s, l_s, acc_s, pos_s, q_s = refs[2 * npg + 1:]
    pp = pl.program_id(1)
    nh = q_ref.shape[1]
    nr, nl = 2 * nh, page * nh
    slope2 = jnp.concatenate([slope_ref[...], slope_ref[...]], axis=0)[:, 0:1]

    @pl.when(pp == 0)
    def _():
        q, kn, vn = q_ref[0], kn_ref[0], vn_ref[0]
        zeros = jnp.zeros((nh, dh), F32)
        q_s[...] = jnp.concatenate([jnp.concatenate([q[:, :dh], zeros], axis=1),
                                    jnp.concatenate([zeros, q[:, dh:]], axis=1)], axis=0)
        lane = lax.broadcasted_iota(jnp.int32, (nr, nl), 1)
        row = lax.broadcasted_iota(jnp.int32, (nr, nl), 0)
        same_head = jnp.bitwise_and(lane, nh - 1) == jnp.bitwise_and(row, nh - 1)
        token = lax.shift_right_logical(lane, nh.bit_length() - 1).astype(F32)
        pos_s[...] = jnp.where(same_head, slope2 * token, -jnp.inf)
        prod = q * kn
        m_s[...] = jnp.concatenate([jnp.sum(prod[:, :dh], axis=-1, keepdims=True),
                                    jnp.sum(prod[:, dh:], axis=-1, keepdims=True)], axis=0)
        l_s[...] = jnp.ones_like(l_s)
        acc_s[...] = jnp.concatenate([vn, vn], axis=0)

    qb = q_s[...]
    scores = []
    mx = None
    for i in range(npg):
        first_pos = (pp * npg + i) * page - past
        s = lax.dot_general(qb, k_refs[i][...].reshape(nl, 2 * dh), (((1,), (1,)), ((), ())),
                            preferred_element_type=F32)
        s = s + pos_s[...] + slope2 * first_pos.astype(F32)
        scores.append(s)
        cur = jnp.max(s, axis=-1, keepdims=True)
        mx = cur if mx is None else jnp.maximum(mx, cur)
    m_prev = m_s[...]
    m_new = jnp.maximum(m_prev, mx)
    alpha = jnp.exp2(m_prev - m_new)
    lsum = jnp.zeros_like(m_new)
    pv = jnp.zeros((nr, dv), F32)
    for i in range(npg):
        p = jnp.exp2(scores[i] - m_new)
        lsum = lsum + jnp.sum(p, axis=-1, keepdims=True)
        pv = pv + jnp.dot(p, v_refs[i][...].reshape(nl, dv), preferred_element_type=F32)
    l_s[...] = alpha * l_s[...] + lsum
    acc_s[...] = alpha * acc_s[...] + pv
    m_s[...] = m_new

    @pl.when(pp == pl.num_programs(1) - 1)
    def _():
        lam = _diff_lambda(lamv_ref, lam_init)
        a, l = acc_s[...], l_s[...]
        o = a[:nh] / l[:nh] - lam * (a[nh:] / l[nh:])
        o = o * lax.rsqrt(jnp.mean(o * o, axis=-1, keepdims=True) + LN_EPS) * sg_ref[...]
        o_ref[0] = o * (1.0 - lam_init)


def _attn_step(q, kn, vn, cache_k, cache_v, layer_j, page_table, slopes_rep, lamv, subln_g,
               nh, dh, dv, lam_init, pages_per_step=8):
    bsz, n_pages = page_table.shape
    page = cache_k.shape[2]
    past = n_pages * page
    assert nh & (nh - 1) == 0, "the (token, head) lane split uses bit masks"
    pps = pages_per_step if n_pages % pages_per_step == 0 else 1
    kern = functools.partial(_attn_step_kernel, dh=dh, dv=dv, page=page, pages_per_step=pps,
                             past=past, lam_init=lam_init)

    def page_spec(i, width):
        return pl.BlockSpec((None, None, page, nh, width),
                            lambda b, pp, pt: (layer_j, pt[b * n_pages + pp * pps + i], 0, 0, 0))

    tok_spec = lambda width: pl.BlockSpec((1, nh, width), lambda b, pp, pt: (b, 0, 0))
    grid_spec = pltpu.PrefetchScalarGridSpec(
        num_scalar_prefetch=1,
        grid=(bsz, n_pages // pps),
        in_specs=[tok_spec(2 * dh), tok_spec(2 * dh), tok_spec(dv),
                  pl.BlockSpec((nh, LANES), lambda b, pp, pt: (0, 0)),
                  pl.BlockSpec((4, dh), lambda b, pp, pt: (0, 0)),
                  pl.BlockSpec((1, dv), lambda b, pp, pt: (0, 0))]
                 + [page_spec(i, 2 * dh) for i in range(pps)]
                 + [page_spec(i, dv) for i in range(pps)],
        out_specs=tok_spec(dv),
        scratch_shapes=[pltpu.VMEM((2 * nh, 1), F32),
                        pltpu.VMEM((2 * nh, 1), F32),
                        pltpu.VMEM((2 * nh, dv), F32),
                        pltpu.VMEM((2 * nh, page * nh), F32),
                        pltpu.VMEM((2 * nh, 2 * dh), F32)],
    )
    return pl.pallas_call(
        kern,
        grid_spec=grid_spec,
        out_shape=jax.ShapeDtypeStruct((bsz, nh, dv), F32),
        compiler_params=_params("parallel", "arbitrary"),
        name="attn_step",
    )(page_table.reshape(-1), q, kn, vn, slopes_rep, lamv, subln_g,
      *([cache_k] * pps), *([cache_v] * pps))


def kernel(x_prompt, x_sample, state_mlstm_c, state_mlstm_n, state_mlstm_m, state_conv, cache_k, cache_v, page_table, w_in_even, b_gate_even, conv_w_even, w_out_even, w_qkv_odd, w_o_odd, lambda_q1, lambda_k1, lambda_q2, lambda_k2, subln_g, w_ffn_gate, w_ffn_up, w_ffn_down, ln_mix_g, ln_mix_b, ln_ffn_g, ln_ffn_b):
    bp, t, d = x_prompt.shape
    bs = x_sample.shape[0]
    assert x_sample.shape[1] == 1, "the sample group decodes one token per sequence"
    depth = w_ffn_gate.shape[0]
    nh_a, dk_a, dv_a = state_mlstm_c.shape[2:]
    dm = nh_a * dk_a
    dc = state_conv.shape[-1]
    nh_c, dv_c = cache_v.shape[3:]
    dh_c = cache_k.shape[4] // 2
    alpha = (2 * depth) ** 0.25
    chunk = math.gcd(t, 256)

    xp = x_prompt.reshape(bp * t, d)
    xs = x_sample.reshape(bs, d)
    xp_b = xs_b = None
    slopes = jnp.exp2(-8.0 * jnp.arange(1, nh_c + 1, dtype=F32) / nh_c)
    slopes_rep = jnp.broadcast_to((slopes * LOG2E)[:, None], (nh_c, LANES))

    wg_all, wu_all, wd_all = w_ffn_gate.astype(BF16), w_ffn_up.astype(BF16), w_ffn_down.astype(BF16)

    pc, pn, pm, pconv, pk, pv = [], [], [], [], [], []
    sc, sn, sm, sconv, sk, sv = [], [], [], [], [], []
    for layer in range(depth):
        j = layer // 2
        g_mix, b_mix = ln_mix_g[layer][None], ln_mix_b[layer][None]
        g_ffn, b_ffn = ln_ffn_g[layer][None], ln_ffn_b[layer][None]
        if layer % 2 == 0:
            w_in = w_in_even[j]
            w_qkv = w_in[:, :3 * dm].astype(BF16)
            gate_lo = 4 * dm
            w_gate = jnp.pad(w_in[:, gate_lo:gate_lo + 2 * nh_a], ((0, 0), (0, LANES - 2 * nh_a))).astype(BF16)
            w_rest = jnp.concatenate([w_in[:, 3 * dm:4 * dm], w_in[:, gate_lo + 2 * nh_a:]], axis=1).astype(BF16)
            w_out = w_out_even[j].astype(BF16)
            b_gate_pad = jnp.pad(b_gate_even[j], (0, LANES - 2 * nh_a))[None]
            conv_w = conv_w_even[j]

            x_in = xp if xp_b is None else xp_b
            (qkv,) = _mm(x_in, w_qkv, [BF16])
            rest, gates = _mm(x_in, w_rest, [F32], w_side=w_gate)
            mix, c1, n1, m1, cv1 = _evenmix_prompt(qkv, rest, gates, b_gate_pad, conv_w,
                                                   bp, t, nh_a, dk_a, dv_a, dc, chunk)
            xp, xp_b = _mm_ln(mix, w_out, xp, g_mix, b_mix, alpha)
            pc.append(c1); pn.append(n1.reshape(bp, nh_a, dk_a)); pm.append(m1[:, 0, :nh_a]); pconv.append(cv1)

            x_in = xs if xs_b is None else xs_b
            (qkv,) = _mm(x_in, w_qkv, [F32])
            rest, gates = _mm(x_in, w_rest, [F32], w_side=w_gate)
            conv_in = rest[:, dm:].reshape(bs, 3, dc)
            m0_pad = jnp.pad(state_mlstm_m[j], ((0, 0), (0, LANES - nh_a)))[:, None, :]
            mix, c2, n2, m2, cv2 = _evenmix_step(
                qkv[:, :dm], qkv[:, dm:2 * dm], qkv[:, 2 * dm:], rest[:, :dm], conv_in,
                gates[:, None, :], b_gate_pad, conv_w, state_mlstm_c[j],
                state_mlstm_n[j][:, :, None, :], m0_pad, state_conv[j], nh_a, dk_a, dv_a, dc)
            xs, xs_b = _mm_ln(mix.reshape(bs, dm + dc).astype(BF16), w_out, xs, g_mix, b_mix, alpha)
            sc.append(c2); sn.append(n2.reshape(bs, nh_a, dk_a)); sm.append(m2[:, 0, :nh_a]); sconv.append(cv2)
        else:
            lam_init = 0.8 - 0.6 * math.exp(-0.3 * layer)
            d_attn = nh_c * dv_c
            w_qkv = w_qkv_odd[j].astype(BF16)
            w_o = w_o_odd[j].astype(BF16)
            lamv = jnp.stack([lambda_q1[j], lambda_k1[j], lambda_q2[j], lambda_k2[j]])
            sg = subln_g[j][None]
            q_scale = dh_c ** -0.5 * LOG2E

            x_in = xp if xp_b is None else xp_b
            (q_b,) = _mm(x_in, w_qkv, [BF16], scale=q_scale, col0=0, n=d_attn, tn_pref=2048)
            k_f, k_b = _mm(x_in, w_qkv, [F32, BF16], col0=d_attn, n=d_attn, tn_pref=2048)
            v_f, v_b = _mm(x_in, w_qkv, [F32, BF16], col0=2 * d_attn, n=d_attn, tn_pref=2048)
            att = _attn_prompt(q_b, k_b, v_b, slopes, lamv, sg, bp, t, nh_c, dh_c, dv_c, lam_init)
            xp, xp_b = _mm_ln(att, w_o, xp, g_mix, b_mix, alpha)
            pk.append(k_f.reshape(bp, t, nh_c, 2 * dh_c)); pv.append(v_f.reshape(bp, t, nh_c, dv_c))

            x_in = xs if xs_b is None else xs_b
            (q_s,) = _mm(x_in, w_qkv, [F32], scale=q_scale, col0=0, n=d_attn)
            (k_s,) = _mm(x_in, w_qkv, [F32], col0=d_attn, n=d_attn)
            (v_s,) = _mm(x_in, w_qkv, [F32], col0=2 * d_attn, n=d_attn)
            att = _attn_step(q_s.reshape(bs, nh_c, 2 * dh_c), k_s.reshape(bs, nh_c, 2 * dh_c),
                             v_s.reshape(bs, nh_c, dv_c), cache_k, cache_v, j, page_table,
                             slopes_rep, lamv, sg, nh_c, dh_c, dv_c, lam_init)
            xs, xs_b = _mm_ln(att.reshape(bs, d_attn).astype(BF16), w_o, xs, g_mix, b_mix, alpha)
            sk.append(k_s.reshape(bs, 1, nh_c, 2 * dh_c)); sv.append(v_s.reshape(bs, 1, nh_c, dv_c))

        xp, xp_b = _ffn(xp_b, wg_all, wu_all, wd_all, layer, xp, g_ffn, b_ffn, alpha)
        xs, xs_b = _ffn(xs_b, wg_all, wu_all, wd_all, layer, xs, g_ffn, b_ffn, alpha)

    return (xp.reshape(bp, t, d), xs.reshape(bs, 1, d),
            jnp.stack(pc), jnp.stack(pn), jnp.stack(pm), jnp.stack(pconv), jnp.stack(pk), jnp.stack(pv),
            jnp.stack(sc), jnp.stack(sn), jnp.stack(sm), jnp.stack(sconv), jnp.stack(sk), jnp.stack(sv))
```

```python
import functools
import math

import jax
import jax.numpy as jnp
from jax import lax
from jax.experimental import pallas as pl
from jax.experimental.pallas import tpu as pltpu

F32 = jnp.float32
BF16 = jnp.bfloat16

LN_EPS = 1e-5
CONV_W = 3
LOG2E = math.log2(math.e)
V7X_VMEM_LIMIT_BYTES = 56 * 1024 * 1024
LANES = 128
SUBLANES = 8


def _params(*sem):
    return pltpu.CompilerParams(dimension_semantics=sem, vmem_limit_bytes=V7X_VMEM_LIMIT_BYTES)


def _layer_norm(y, g, b):
    mu = jnp.mean(y, axis=-1, keepdims=True)
    d = y - mu
    var = jnp.mean(d * d, axis=-1, keepdims=True)
    return d * lax.rsqrt(var + LN_EPS) * g + b


def _log_sigmoid(x):
    return jnp.minimum(x, 0.0) - jnp.log1p(jnp.exp(-jnp.abs(x)))


def _tile(n, pref):
    if n <= pref:
        return n
    t = pref
    while n % t:
        t //= 2
    return t


def _mm_kernel(x_ref, w_ref, *refs, scale, cast_x, n_out, has_side):
    refs = list(refs)
    ws_ref = refs.pop(0) if has_side else None
    out_refs = [refs.pop(0) for _ in range(n_out)]
    side_ref = refs.pop(0) if has_side else None
    first = pl.program_id(1) == 0
    if cast_x:
        xb_ref = refs.pop(0)

        @pl.when(first)
        def _():
            xb_ref[...] = x_ref[...].astype(BF16)

        x = xb_ref[...]
    else:
        x = x_ref[...]
    if has_side:
        @pl.when(first)
        def _():
            side_ref[...] = jnp.dot(x, ws_ref[...], preferred_element_type=F32)

    acc = jnp.dot(x, w_ref[...], preferred_element_type=F32)
    if scale is not None:
        acc = acc * scale
    for o_ref in out_refs:
        o_ref[...] = acc.astype(o_ref.dtype)


def _mm(x, w, out_dtypes, scale=None, col0=0, n=None, w_side=None, tm_pref=1024, tn_pref=1024):
    m, k = x.shape
    n = w.shape[1] if n is None else n
    tm, tn = _tile(m, tm_pref), _tile(n, tn_pref)
    assert col0 % tn == 0
    j0 = col0 // tn
    cast_x = x.dtype != BF16
    has_side = w_side is not None
    kern = functools.partial(_mm_kernel, scale=scale, cast_x=cast_x, n_out=len(out_dtypes), has_side=has_side)
    in_specs = [pl.BlockSpec((tm, k), lambda i, j: (i, 0)),
                pl.BlockSpec((k, tn), lambda i, j: (0, j + j0))]
    out_specs = [pl.BlockSpec((tm, tn), lambda i, j: (i, j)) for _ in out_dtypes]
    out_shape = [jax.ShapeDtypeStruct((m, n), dt) for dt in out_dtypes]
    operands = [x, w]
    if has_side:
        n_side = w_side.shape[1]
        in_specs.append(pl.BlockSpec((k, n_side), lambda i, j: (0, 0)))
        out_specs.append(pl.BlockSpec((tm, n_side), lambda i, j: (i, 0)))
        out_shape.append(jax.ShapeDtypeStruct((m, n_side), F32))
        operands.append(w_side)
    outs = pl.pallas_call(
        kern,
        grid=(m // tm, n // tn),
        in_specs=in_specs,
        out_specs=out_specs,
        out_shape=out_shape,
        scratch_shapes=[pltpu.VMEM((tm, k), BF16)] if cast_x else [],
        compiler_params=_params("parallel", "arbitrary"),
        name="mm",
    )(*operands)
    return outs


def _mm_ln_kernel(x_ref, w_ref, res_ref, g_ref, b_ref, of_ref, ob_ref, *, alpha, nsub):
    rows = x_ref.shape[0] // nsub
    for r in range(nsub):
        sl = slice(r * rows, (r + 1) * rows)
        f = jnp.dot(x_ref[sl, :], w_ref[...], preferred_element_type=F32)
        y = _layer_norm(alpha * res_ref[sl, :] + f, g_ref[...], b_ref[...])
        of_ref[sl, :] = y
        ob_ref[sl, :] = y.astype(BF16)


def _mm_ln(x, w, res, g, b, alpha, tm_pref=512):
    m, k = x.shape
    d = w.shape[1]
    tm = _tile(m, tm_pref)
    return pl.pallas_call(
        functools.partial(_mm_ln_kernel, alpha=alpha, nsub=2 if tm % 512 == 0 else 1),
        grid=(m // tm,),
        in_specs=[pl.BlockSpec((tm, k), lambda i: (i, 0)),
                  pl.BlockSpec((k, d), lambda i: (0, 0)),
                  pl.BlockSpec((tm, d), lambda i: (i, 0)),
                  pl.BlockSpec((1, d), lambda i: (0, 0)),
                  pl.BlockSpec((1, d), lambda i: (0, 0))],
        out_specs=[pl.BlockSpec((tm, d), lambda i: (i, 0)),
                   pl.BlockSpec((tm, d), lambda i: (i, 0))],
        out_shape=[jax.ShapeDtypeStruct((m, d), F32), jax.ShapeDtypeStruct((m, d), BF16)],
        compiler_params=_params("parallel"),
        name="mm_ln",
    )(x, w, res, g, b)


def _ffn_kernel(x_ref, wg_ref, wu_ref, wd_ref, res_ref, g_ref, b_ref, of_ref, ob_ref, acc_ref, *, alpha):
    f = pl.program_id(1)

    @pl.when(f == 0)
    def _():
        acc_ref[...] = jnp.zeros_like(acc_ref)

    x = x_ref[...]
    gate = jnp.dot(x, wg_ref[...], preferred_element_type=F32)
    up = jnp.dot(x, wu_ref[...], preferred_element_type=F32)
    h = (gate * jax.nn.sigmoid(gate) * up).astype(BF16)
    acc_ref[...] += jnp.dot(h, wd_ref[...], preferred_element_type=F32)

    @pl.when(f == pl.num_programs(1) - 1)
    def _():
        y = _layer_norm(alpha * res_ref[...] + acc_ref[...], g_ref[...], b_ref[...])
        of_ref[...] = y
        ob_ref[...] = y.astype(BF16)


def _ffn(x, wg, wu, wd, layer, res, g, b, alpha, tm_pref=512, tf_pref=512):
    m, d = x.shape
    ff = wg.shape[2]
    tm, tf = _tile(m, tm_pref), _tile(ff, tf_pref)
    return pl.pallas_call(
        functools.partial(_ffn_kernel, alpha=alpha),
        grid=(m // tm, ff // tf),
        in_specs=[pl.BlockSpec((tm, d), lambda i, f: (i, 0)),
                  pl.BlockSpec((None, d, tf), lambda i, f: (layer, 0, f)),
                  pl.BlockSpec((None, d, tf), lambda i, f: (layer, 0, f)),
                  pl.BlockSpec((None, tf, d), lambda i, f: (layer, f, 0)),
                  pl.BlockSpec((tm, d), lambda i, f: (i, 0)),
                  pl.BlockSpec((1, d), lambda i, f: (0, 0)),
                  pl.BlockSpec((1, d), lambda i, f: (0, 0))],
        out_specs=[pl.BlockSpec((tm, d), lambda i, f: (i, 0)),
                   pl.BlockSpec((tm, d), lambda i, f: (i, 0))],
        out_shape=[jax.ShapeDtypeStruct((m, d), F32), jax.ShapeDtypeStruct((m, d), BF16)],
        scratch_shapes=[pltpu.VMEM((tm, d), F32)],
        compiler_params=_params("parallel", "arbitrary"),
        name="ffn",
    )(x, wg, wu, wd, res, g, b)


def _evenmix_kernel(qkv_ref, rest_ref, gate_ref, bg_ref, cw_ref,
                    mix_ref, c_out, n_out, m_out, cv_out,
                    c_s, n_s, m_s, u_s, *, nh, dk, dv, dc, chunk):
    c = pl.program_id(1)
    last = pl.num_programs(1) - 1
    dm = nh * dk
    L = chunk

    @pl.when(c == 0)
    def _():
        c_s[...] = jnp.zeros_like(c_s)
        n_s[...] = jnp.zeros_like(n_s)
        m_s[...] = jnp.zeros_like(m_s)
        u_s[...] = jnp.zeros_like(u_s)

    g = gate_ref[...] + bg_ref[...]
    lane = lax.broadcasted_iota(jnp.int32, (L, LANES), 1)
    row = lax.broadcasted_iota(jnp.int32, (L, LANES), 0)
    bcum = _log_sigmoid(g)
    s = 1
    while s < L:
        bcum = bcum + jnp.where(row >= s, pltpu.roll(bcum, s, 0), 0.0)
        s *= 2
    z = jnp.where(lane < nh, g, bcum)
    zt = z.T

    tr = lax.broadcasted_iota(jnp.int32, (L, L), 0)
    tc = lax.broadcasted_iota(jnp.int32, (L, L), 1)
    causal = tr >= tc
    lane1 = lax.broadcasted_iota(jnp.int32, (1, LANES), 1)
    m_vec = jnp.zeros((1, LANES), F32)

    for h in range(nh):
        qc = qkv_ref[:, h * dk:(h + 1) * dk]
        kc = qkv_ref[:, dm + h * dk:dm + (h + 1) * dk] * (dk ** -0.5)
        vc = qkv_ref[:, 2 * dm + h * dv:2 * dm + (h + 1) * dv]
        ig_col = z[:, h:h + 1]
        b_col = z[:, nh + h:nh + h + 1]
        ig_row = zt[h:h + 1, :]
        b_row = zt[nh + h:nh + h + 1, :]
        m_prev = m_s[h][:, 0:1]
        cmat = c_s[h]
        n_row = n_s[h]

        dmat = jnp.where(causal, b_col - b_row + ig_row, -jnp.inf)
        inter = b_col + m_prev
        m_t = jnp.maximum(inter, jnp.max(dmat, axis=-1, keepdims=True))
        w_intra = jnp.exp(dmat - m_t)
        w_inter = jnp.exp(inter - m_t)
        sc = lax.dot_general(qc, kc, (((1,), (1,)), ((), ())), preferred_element_type=F32) * w_intra
        num = (w_inter * jnp.dot(qc, cmat.astype(BF16), preferred_element_type=F32)
               + jnp.dot(sc.astype(BF16), vc, preferred_element_type=F32))
        qn = jnp.sum(qc.astype(F32) * n_row, axis=-1, keepdims=True)
        den = w_inter * qn + jnp.sum(sc, axis=-1, keepdims=True)
        out = num * (1.0 / jnp.maximum(jnp.abs(den), jnp.exp(-m_t)))

        m_new = m_t[L - 1:L, :]
        b_last = b_col[L - 1:L, :]
        g_inter = jnp.exp(b_last + m_prev - m_new)
        g_s = jnp.exp(b_last - b_col + ig_col - m_new)
        kg = kc.astype(F32) * g_s
        c_s[h] = g_inter * cmat + jnp.dot(kg.T.astype(BF16), vc, preferred_element_type=F32)
        n_s[h] = g_inter * n_row + jnp.sum(kg, axis=0, keepdims=True)
        m_s[h] = jnp.broadcast_to(m_new, (1, LANES))
        m_vec = jnp.where(lane1 == h, m_new, m_vec)

        o_gate = rest_ref[:, h * dv:(h + 1) * dv]
        mix_ref[:, h * dv:(h + 1) * dv] = (jax.nn.sigmoid(o_gate) * out).astype(BF16)

    base = nh * dv
    gb = rest_ref[:, base:base + dc]
    u = rest_ref[:, base + dc:base + 2 * dc] * rest_ref[:, base + 2 * dc:base + 3 * dc]
    prev2 = u_s[0:1, :]
    prev1 = u_s[1:2, :]
    r1, r2 = pltpu.roll(u, 1, 0), pltpu.roll(u, 2, 0)
    row8 = lax.broadcasted_iota(jnp.int32, (SUBLANES, dc), 0)
    u1 = jnp.concatenate([jnp.where(row8 == 0, prev1, r1[:SUBLANES]), r1[SUBLANES:]], axis=0)
    u2 = jnp.concatenate([jnp.where(row8 == 0, prev2, jnp.where(row8 == 1, prev1, r2[:SUBLANES])),
                          r2[SUBLANES:]], axis=0)
    yc = gb * (cw_ref[0:1, :] * u2 + cw_ref[1:2, :] * u1 + cw_ref[2:3, :] * u)
    mix_ref[:, dm:dm + dc] = yc.astype(BF16)
    u_s[0:1, :] = u[L - 2:L - 1, :]
    u_s[1:2, :] = u[L - 1:L, :]

    @pl.when(c == last)
    def _():
        c_out[0] = c_s[...]
        n_out[0] = n_s[...]
        m_out[0] = m_vec
        cv_out[0, 0:1, :] = u[L - 2:L - 1, :]
        cv_out[0, 1:2, :] = u[L - 1:L, :]


def _evenmix_prompt(qkv, rest, gates, b_gate_pad, conv_w, bsz, t, nh, dk, dv, dc, chunk):
    nc = t // chunk
    dm = nh * dk
    kern = functools.partial(_evenmix_kernel, nh=nh, dk=dk, dv=dv, dc=dc, chunk=chunk)
    return pl.pallas_call(
        kern,
        grid=(bsz, nc),
        in_specs=[pl.BlockSpec((chunk, 3 * dm), lambda b, c: (b * nc + c, 0)),
                  pl.BlockSpec((chunk, dm + 3 * dc), lambda b, c: (b * nc + c, 0)),
                  pl.BlockSpec((chunk, LANES), lambda b, c: (b * nc + c, 0)),
                  pl.BlockSpec((1, LANES), lambda b, c: (0, 0)),
                  pl.BlockSpec((CONV_W, dc), lambda b, c: (0, 0))],
        out_specs=[pl.BlockSpec((chunk, dm + dc), lambda b, c: (b * nc + c, 0)),
                   pl.BlockSpec((1, nh, dk, dv), lambda b, c: (b, 0, 0, 0)),
                   pl.BlockSpec((1, nh, 1, dk), lambda b, c: (b, 0, 0, 0)),
                   pl.BlockSpec((1, 1, LANES), lambda b, c: (b, 0, 0)),
                   pl.BlockSpec((1, CONV_W - 1, dc), lambda b, c: (b, 0, 0))],
        out_shape=[jax.ShapeDtypeStruct((bsz * t, dm + dc), BF16),
                   jax.ShapeDtypeStruct((bsz, nh, dk, dv), F32),
                   jax.ShapeDtypeStruct((bsz, nh, 1, dk), F32),
                   jax.ShapeDtypeStruct((bsz, 1, LANES), F32),
                   jax.ShapeDtypeStruct((bsz, CONV_W - 1, dc), F32)],
        scratch_shapes=[pltpu.VMEM((nh, dk, dv), F32),
                        pltpu.VMEM((nh, 1, dk), F32),
                        pltpu.VMEM((nh, 1, LANES), F32),
                        pltpu.VMEM((SUBLANES, dc), F32)],
        compiler_params=_params("parallel", "arbitrary"),
        name="evenmix_prompt",
    )(qkv, rest, gates, b_gate_pad, conv_w)


def _evenmix_step_kernel(q_ref, k_ref, v_ref, o_ref, conv_ref, gate_ref, bg_ref, cw_ref,
                         c_ref, n_ref, m_ref, cv_ref,
                         mix_ref, c_out, n_out, m_out, cv_out, *, nh, dk, dv, dc, nseq):
    dm = nh * dk
    scale = dk ** -0.5
    lane1 = lax.broadcasted_iota(jnp.int32, (1, LANES), 1)
    row128 = lax.broadcasted_iota(jnp.int32, (LANES, dk), 0)
    for b in range(nseq):
        q_rows = q_ref[b]
        k_rows = k_ref[b] * scale
        tile = jnp.zeros((LANES, dk), F32)
        for h in range(nh):
            tile = jnp.where(row128 == h, q_rows[h:h + 1, :], tile)
            tile = jnp.where(row128 == nh + h, k_rows[h:h + 1, :], tile)
        cols = tile.T
        g = gate_ref[b] + bg_ref[...]
        lf_all = _log_sigmoid(g)
        m_all = m_ref[b]
        m_vec = jnp.zeros((1, LANES), F32)
        for h in range(nh):
            q_col = cols[:, h:h + 1]
            k_col = cols[:, nh + h:nh + h + 1]
            q_row = q_rows[h:h + 1, :]
            k_row = k_rows[h:h + 1, :]
            v_row = v_ref[b, h:h + 1, :]
            ig = g[:, h:h + 1]
            lf = lf_all[:, nh + h:nh + h + 1]
            m_prev = m_all[:, h:h + 1]
            cmat = c_ref[b, h]
            n_row = n_ref[b, h]

            inter = lf + m_prev
            m_t = jnp.maximum(inter, ig)
            w_intra = jnp.exp(ig - m_t)
            w_inter = jnp.exp(inter - m_t)
            sc = jnp.sum(q_row * k_row, axis=-1, keepdims=True) * w_intra
            num = w_inter * jnp.sum(q_col * cmat, axis=0, keepdims=True) + sc * v_row
            den = w_inter * jnp.sum(q_row * n_row, axis=-1, keepdims=True) + sc
            out = num / jnp.maximum(jnp.abs(den), jnp.exp(-m_t))

            g_inter = jnp.exp(lf + m_prev - m_t)
            g_s = jnp.exp(ig - m_t)
            c_out[b, h] = g_inter * cmat + (g_s * k_col) * v_row
            n_out[b, h] = g_inter * n_row + g_s * k_row
            m_vec = jnp.where(lane1 == h, m_t, m_vec)
            mix_ref[b, :, h * dv:(h + 1) * dv] = jax.nn.sigmoid(o_ref[b, h:h + 1, :]) * out
        m_out[b] = m_vec

        gb = conv_ref[b, 0:1, :]
        u = conv_ref[b, 1:2, :] * conv_ref[b, 2:3, :]
        prev2 = cv_ref[b, 0:1, :]
        prev1 = cv_ref[b, 1:2, :]
        yc = gb * (cw_ref[0:1, :] * prev2 + cw_ref[1:2, :] * prev1 + cw_ref[2:3, :] * u)
        mix_ref[b, :, dm:dm + dc] = yc
        cv_out[b, 0:1, :] = prev1
        cv_out[b, 1:2, :] = u


def _evenmix_step(q, k, v, o, conv_in, gates, b_gate_pad, conv_w, c0, n0, m0_pad, cv0, nh, dk, dv, dc):
    bsz = q.shape[0]
    dm = nh * dk
    assert 2 * nh <= LANES
    nseq = _tile(bsz, 4)
    head_spec = lambda w: pl.BlockSpec((nseq, nh, w), lambda b: (b, 0, 0))
    kern = functools.partial(_evenmix_step_kernel, nh=nh, dk=dk, dv=dv, dc=dc, nseq=nseq)
    return pl.pallas_call(
        kern,
        grid=(bsz // nseq,),
        in_specs=[head_spec(dk), head_spec(dk), head_spec(dv), head_spec(dv),
                  pl.BlockSpec((nseq, 3, dc), lambda b: (b, 0, 0)),
                  pl.BlockSpec((nseq, 1, LANES), lambda b: (b, 0, 0)),
                  pl.BlockSpec((1, LANES), lambda b: (0, 0)),
                  pl.BlockSpec((CONV_W, dc), lambda b: (0, 0)),
                  pl.BlockSpec((nseq, nh, dk, dv), lambda b: (b, 0, 0, 0)),
                  pl.BlockSpec((nseq, nh, 1, dk), lambda b: (b, 0, 0, 0)),
                  pl.BlockSpec((nseq, 1, LANES), lambda b: (b, 0, 0)),
                  pl.BlockSpec((nseq, CONV_W - 1, dc), lambda b: (b, 0, 0))],
        out_specs=[pl.BlockSpec((nseq, 1, dm + dc), lambda b: (b, 0, 0)),
                   pl.BlockSpec((nseq, nh, dk, dv), lambda b: (b, 0, 0, 0)),
                   pl.BlockSpec((nseq, nh, 1, dk), lambda b: (b, 0, 0, 0)),
                   pl.BlockSpec((nseq, 1, LANES), lambda b: (b, 0, 0)),
                   pl.BlockSpec((nseq, CONV_W - 1, dc), lambda b: (b, 0, 0))],
        out_shape=[jax.ShapeDtypeStruct((bsz, 1, dm + dc), F32),
                   jax.ShapeDtypeStruct((bsz, nh, dk, dv), F32),
                   jax.ShapeDtypeStruct((bsz, nh, 1, dk), F32),
                   jax.ShapeDtypeStruct((bsz, 1, LANES), F32),
                   jax.ShapeDtypeStruct((bsz, CONV_W - 1, dc), F32)],
        compiler_params=_params("parallel"),
        name="evenmix_step",
    )(q.reshape(bsz, nh, dk), k.reshape(bsz, nh, dk), v.reshape(bsz, nh, dv), o.reshape(bsz, nh, dv),
      conv_in, gates, b_gate_pad, conv_w, c0, n0, m0_pad, cv0)


def _diff_lambda(lamv_ref, lam_init):
    a = jnp.sum(lamv_ref[0:1, :] * lamv_ref[1:2, :], axis=-1, keepdims=True)
    b = jnp.sum(lamv_ref[2:3, :] * lamv_ref[3:4, :], axis=-1, keepdims=True)
    return jnp.exp(a) - jnp.exp(b) + lam_init


def _attn_prompt_kernel(slope_ref, q_ref, k_ref, v_ref, lamv_ref, sg_ref, o_ref, kb_s,
                        *, dh, dv, blk, lam_init):
    h = pl.program_id(1)
    t_all = k_ref.shape[0]
    nq = t_all // blk

    pos = (lax.broadcasted_iota(jnp.int32, (t_all, LANES), 0).astype(F32) * (slope_ref[h] * LOG2E))
    lane = lax.broadcasted_iota(jnp.int32, (t_all, LANES), 1)
    hi = pos.astype(BF16).astype(F32)
    r1 = pos - hi
    mid = r1.astype(BF16).astype(F32)
    lo = r1 - mid
    pieces = jnp.where(lane == 0, hi, jnp.where(lane == 1, mid, jnp.where(lane == 2, lo, 0.0)))
    kb_s[...] = pieces.astype(BF16)

    half = blk // 2
    ones_cols = {n: jnp.where(lax.broadcasted_iota(jnp.int32, (n, LANES), 1) < 3, 1.0, 0.0).astype(BF16)
                 for n in (half, blk)}
    tri_top = (lax.broadcasted_iota(jnp.int32, (half, half), 0)
               >= lax.broadcasted_iota(jnp.int32, (half, half), 1))
    tri_bot = (lax.broadcasted_iota(jnp.int32, (half, blk), 0) + half
               >= lax.broadcasted_iota(jnp.int32, (half, blk), 1))
    lam = _diff_lambda(lamv_ref, lam_init)

    def scores(qs, ks):
        kb = kb_s[ks, :]
        out = []
        for c in range(2):
            qa = jnp.concatenate([q_ref[qs, c * dh:(c + 1) * dh], ones_cols[qs.stop - qs.start]], axis=1)
            ka = jnp.concatenate([k_ref[ks, c * dh:(c + 1) * dh], kb], axis=1)
            out.append(lax.dot_general(qa, ka, (((1,), (1,)), ((), ())), preferred_element_type=F32))
        return out

    def update(state, s, v):
        nrow = s.shape[0]
        slabs = [s[:, j * LANES:(j + 1) * LANES] for j in range(s.shape[1] // LANES)]
        m_cur = jnp.max(functools.reduce(jnp.maximum, slabs), axis=-1, keepdims=True)
        m_new = jnp.broadcast_to(m_cur, (nrow, LANES)) if state is None else jnp.maximum(state[0], m_cur)
        ps = [jnp.exp2(sl - m_new) for sl in slabs]
        psum = jnp.sum(functools.reduce(lambda a, b: a + b, ps), axis=-1, keepdims=True)
        pv = jnp.dot(jnp.concatenate([x.astype(BF16) for x in ps], axis=1), v, preferred_element_type=F32)
        if state is None:
            return m_new, jnp.broadcast_to(psum, (nrow, LANES)), pv
        alpha = jnp.exp2(state[0] - m_new)
        acc = jnp.concatenate([alpha * state[2][:, e * LANES:(e + 1) * LANES] for e in range(dv // LANES)],
                              axis=1) + pv
        return m_new, alpha * state[1] + psum, acc

    for qi in range(nq):
        q0 = qi * blk
        rows = slice(q0, q0 + blk)
        state = [None, None]
        s_next = scores(rows, slice(0, blk)) if qi > 0 else None
        for ki in range(qi):
            s_cur = s_next
            s_next = scores(rows, slice((ki + 1) * blk, (ki + 2) * blk)) if ki + 1 < qi else None
            v = v_ref[ki * blk:(ki + 1) * blk, :]
            for c in range(2):
                state[c] = update(state[c], s_cur[c], v)
        s_top = scores(slice(q0, q0 + half), slice(q0, q0 + half))
        s_bot = scores(slice(q0 + half, q0 + blk), rows)
        v_d = v_ref[rows, :]
        for c in range(2):
            st = state[c]
            st_top = None if st is None else tuple(x[:half] for x in st)
            st_bot = None if st is None else tuple(x[half:] for x in st)
            st_top = update(st_top, jnp.where(tri_top, s_top[c], -jnp.inf), v_d[:half])
            st_bot = update(st_bot, jnp.where(tri_bot, s_bot[c], -jnp.inf), v_d)
            state[c] = tuple(jnp.concatenate([a, b], axis=0) for a, b in zip(st_top, st_bot))
        inv0, inv1 = 1.0 / state[0][1], lam / state[1][1]
        o = jnp.concatenate([state[0][2][:, e * LANES:(e + 1) * LANES] * inv0
                             - state[1][2][:, e * LANES:(e + 1) * LANES] * inv1
                             for e in range(dv // LANES)], axis=1)
        o = o * lax.rsqrt(jnp.mean(o * o, axis=-1, keepdims=True) + LN_EPS) * sg_ref[...]
        o_ref[qi * blk:(qi + 1) * blk, :] = (o * (1.0 - lam_init)).astype(BF16)


def _attn_prompt(q, k, v, slopes, lamv, subln_g, bsz, t, nh, dh, dv, lam_init, blk_pref=512):
    blk = _tile(t, blk_pref)
    kern = functools.partial(_attn_prompt_kernel, dh=dh, dv=dv, blk=blk, lam_init=lam_init)
    return pl.pallas_call(
        kern,
        grid=(bsz, nh),
        in_specs=[pl.BlockSpec(memory_space=pltpu.SMEM),
                  pl.BlockSpec((t, 2 * dh), lambda b, h: (b, h)),
                  pl.BlockSpec((t, 2 * dh), lambda b, h: (b, h)),
                  pl.BlockSpec((t, dv), lambda b, h: (b, h)),
                  pl.BlockSpec((4, dh), lambda b, h: (0, 0)),
                  pl.BlockSpec((1, dv), lambda b, h: (0, 0))],
        out_specs=pl.BlockSpec((t, dv), lambda b, h: (b, h)),
        out_shape=jax.ShapeDtypeStruct((bsz * t, nh * dv), BF16),
        scratch_shapes=[pltpu.VMEM((t, LANES), BF16)],
        compiler_params=_params("parallel", "parallel"),
        name="attn_prompt",
    )(slopes, q, k, v, lamv, subln_g)


def _attn_step_kernel(pt_ref, q_ref, kn_ref, vn_ref, slope_ref, lamv_ref, sg_ref, *refs,
                      dh, dv, page, pages_per_step, past, lam_init):
    del pt_ref
    npg = pages_per_step
    k_refs = refs[:npg]
    v_refs = refs[npg:2 * npg]
    o_ref = refs[2 * npg]
    m_s, l_s, acc_s, pos_s, q_s = refs[2 * npg + 1:]
    pp = pl.program_id(1)
    nh = q_ref.shape[1]
    nr, nl = 2 * nh, page * nh
    slope2 = jnp.concatenate([slope_ref[...], slope_ref[...]], axis=0)[:, 0:1]

    @pl.when(pp == 0)
    def _():
        q, kn, vn = q_ref[0], kn_ref[0], vn_ref[0]
        zeros = jnp.zeros((nh, dh), F32)
        q_s[...] = jnp.concatenate([jnp.concatenate([q[:, :dh], zeros], axis=1),
                                    jnp.concatenate([zeros, q[:, dh:]], axis=1)], axis=0)
        lane = lax.broadcasted_iota(jnp.int32, (nr, nl), 1)
        row = lax.broadcasted_iota(jnp.int32, (nr, nl), 0)
        same_head = jnp.bitwise_and(lane, nh - 1) == jnp.bitwise_and(row, nh - 1)
        token = lax.shift_right_logical(lane, nh.bit_length() - 1).astype(F32)
        pos_s[...] = jnp.where(same_head, slope2 * token, -jnp.inf)
        prod = q * kn
        m_s[...] = jnp.concatenate([jnp.sum(prod[:, :dh], axis=-1, keepdims=True),
                                    jnp.sum(prod[:, dh:], axis=-1, keepdims=True)], axis=0)
        l_s[...] = jnp.ones_like(l_s)
        acc_s[...] = jnp.concatenate([vn, vn], axis=0)

    qb = q_s[...]
    scores = []
    mx = None
    for i in range(npg):
        first_pos = (pp * npg + i) * page - past
        s = lax.dot_general(qb, k_refs[i][...].reshape(nl, 2 * dh), (((1,), (1,)), ((), ())),
                            preferred_element_type=F32)
        s = s + pos_s[...] + slope2 * first_pos.astype(F32)
        scores.append(s)
        cur = jnp.max(s, axis=-1, keepdims=True)
        mx = cur if mx is None else jnp.maximum(mx, cur)
    m_prev = m_s[...]
    m_new = jnp.maximum(m_prev, mx)
    alpha = jnp.exp2(m_prev - m_new)
    lsum = jnp.zeros_like(m_new)
    pv = jnp.zeros((nr, dv), F32)
    for i in range(npg):
        p = jnp.exp2(scores[i] - m_new)
        lsum = lsum + jnp.sum(p, axis=-1, keepdims=True)
        pv = pv + jnp.dot(p, v_refs[i][...].reshape(nl, dv), preferred_element_type=F32)
    l_s[...] = alpha * l_s[...] + lsum
    acc_s[...] = alpha * acc_s[...] + pv
    m_s[...] = m_new

    @pl.when(pp == pl.num_programs(1) - 1)
    def _():
        lam = _diff_lambda(lamv_ref, lam_init)
        a, l = acc_s[...], l_s[...]
        o = a[:nh] / l[:nh] - lam * (a[nh:] / l[nh:])
        o = o * lax.rsqrt(jnp.mean(o * o, axis=-1, keepdims=True) + LN_EPS) * sg_ref[...]
        o_ref[0] = o * (1.0 - lam_init)


def _attn_step(q, kn, vn, cache_k, cache_v, layer_j, page_table, slopes_rep, lamv, subln_g,
               nh, dh, dv, lam_init, pages_per_step=8):
    bsz, n_pages = page_table.shape
    page = cache_k.shape[2]
    past = n_pages * page
    assert nh & (nh - 1) == 0, "the (token, head) lane split uses bit masks"
    pps = pages_per_step if n_pages % pages_per_step == 0 else 1
    kern = functools.partial(_attn_step_kernel, dh=dh, dv=dv, page=page, pages_per_step=pps,
                             past=past, lam_init=lam_init)

    def page_spec(i, width):
        return pl.BlockSpec((None, None, page, nh, width),
                            lambda b, pp, pt: (layer_j, pt[b * n_pages + pp * pps + i], 0, 0, 0))

    tok_spec = lambda width: pl.BlockSpec((1, nh, width), lambda b, pp, pt: (b, 0, 0))
    grid_spec = pltpu.PrefetchScalarGridSpec(
        num_scalar_prefetch=1,
        grid=(bsz, n_pages // pps),
        in_specs=[tok_spec(2 * dh), tok_spec(2 * dh), tok_spec(dv),
                  pl.BlockSpec((nh, LANES), lambda b, pp, pt: (0, 0)),
                  pl.BlockSpec((4, dh), lambda b, pp, pt: (0, 0)),
                  pl.BlockSpec((1, dv), lambda b, pp, pt: (0, 0))]
                 + [page_spec(i, 2 * dh) for i in range(pps)]
                 + [page_spec(i, dv) for i in range(pps)],
        out_specs=tok_spec(dv),
        scratch_shapes=[pltpu.VMEM((2 * nh, 1), F32),
                        pltpu.VMEM((2 * nh, 1), F32),
                        pltpu.VMEM((2 * nh, dv), F32),
                        pltpu.VMEM((2 * nh, page * nh), F32),
                        pltpu.VMEM((2 * nh, 2 * dh), F32)],
    )
    return pl.pallas_call(
        kern,
        grid_spec=grid_spec,
        out_shape=jax.ShapeDtypeStruct((bsz, nh, dv), F32),
        compiler_params=_params("parallel", "arbitrary"),
        name="attn_step",
    )(page_table.reshape(-1), q, kn, vn, slopes_rep, lamv, subln_g,
      *([cache_k] * pps), *([cache_v] * pps))


def kernel(x_prompt, x_sample, state_mlstm_c, state_mlstm_n, state_mlstm_m, state_conv, cache_k, cache_v, page_table, w_in_even, b_gate_even, conv_w_even, w_out_even, w_qkv_odd, w_o_odd, lambda_q1, lambda_k1, lambda_q2, lambda_k2, subln_g, w_ffn_gate, w_ffn_up, w_ffn_down, ln_mix_g, ln_mix_b, ln_ffn_g, ln_ffn_b):
    bp, t, d = x_prompt.shape
    bs = x_sample.shape[0]
    assert x_sample.shape[1] == 1, "the sample group decodes one token per sequence"
    depth = w_ffn_gate.shape[0]
    nh_a, dk_a, dv_a = state_mlstm_c.shape[2:]
    dm = nh_a * dk_a
    dc = state_conv.shape[-1]
    nh_c, dv_c = cache_v.shape[3:]
    dh_c = cache_k.shape[4] // 2
    alpha = (2 * depth) ** 0.25
    chunk = math.gcd(t, 256)

    xp = x_prompt.reshape(bp * t, d)
    xs = x_sample.reshape(bs, d)
    xp_b = xs_b = None
    slopes = jnp.exp2(-8.0 * jnp.arange(1, nh_c + 1, dtype=F32) / nh_c)
    slopes_rep = jnp.broadcast_to((slopes * LOG2E)[:, None], (nh_c, LANES))

    wg_all, wu_all, wd_all = w_ffn_gate.astype(BF16), w_ffn_up.astype(BF16), w_ffn_down.astype(BF16)

    pc, pn, pm, pconv, pk, pv = [], [], [], [], [], []
    sc, sn, sm, sconv, sk, sv = [], [], [], [], [], []
    for layer in range(depth):
        j = layer // 2
        g_mix, b_mix = ln_mix_g[layer][None], ln_mix_b[layer][None]
        g_ffn, b_ffn = ln_ffn_g[layer][None], ln_ffn_b[layer][None]
        if layer % 2 == 0:
            w_in = w_in_even[j]
            w_qkv = w_in[:, :3 * dm].astype(BF16)
            gate_lo = 4 * dm
            w_gate = jnp.pad(w_in[:, gate_lo:gate_lo + 2 * nh_a], ((0, 0), (0, LANES - 2 * nh_a))).astype(BF16)
            w_rest = jnp.concatenate([w_in[:, 3 * dm:4 * dm], w_in[:, gate_lo + 2 * nh_a:]], axis=1).astype(BF16)
            w_out = w_out_even[j].astype(BF16)
            b_gate_pad = jnp.pad(b_gate_even[j], (0, LANES - 2 * nh_a))[None]
            conv_w = conv_w_even[j]

            x_in = xp if xp_b is None else xp_b
            (qkv,) = _mm(x_in, w_qkv, [BF16], tn_pref=3 * dm // 2)
            rest, gates = _mm(x_in, w_rest, [F32], w_side=w_gate)
            mix, c1, n1, m1, cv1 = _evenmix_prompt(qkv, rest, gates, b_gate_pad, conv_w,
                                                   bp, t, nh_a, dk_a, dv_a, dc, chunk)
            xp, xp_b = _mm_ln(mix, w_out, xp, g_mix, b_mix, alpha)
            pc.append(c1); pn.append(n1.reshape(bp, nh_a, dk_a)); pm.append(m1[:, 0, :nh_a]); pconv.append(cv1)

            x_in = xs if xs_b is None else xs_b
            (qkv,) = _mm(x_in, w_qkv, [F32])
            rest, gates = _mm(x_in, w_rest, [F32], w_side=w_gate)
            conv_in = rest[:, dm:].reshape(bs, 3, dc)
            m0_pad = jnp.pad(state_mlstm_m[j], ((0, 0), (0, LANES - nh_a)))[:, None, :]
            mix, c2, n2, m2, cv2 = _evenmix_step(
                qkv[:, :dm], qkv[:, dm:2 * dm], qkv[:, 2 * dm:], rest[:, :dm], conv_in,
                gates[:, None, :], b_gate_pad, conv_w, state_mlstm_c[j],
                state_mlstm_n[j][:, :, None, :], m0_pad, state_conv[j], nh_a, dk_a, dv_a, dc)
            xs, xs_b = _mm_ln(mix.reshape(bs, dm + dc).astype(BF16), w_out, xs, g_mix, b_mix, alpha)
            sc.append(c2); sn.append(n2.reshape(bs, nh_a, dk_a)); sm.append(m2[:, 0, :nh_a]); sconv.append(cv2)
        else:
            lam_init = 0.8 - 0.6 * math.exp(-0.3 * layer)
            d_attn = nh_c * dv_c
            w_qkv = w_qkv_odd[j].astype(BF16)
            w_o = w_o_odd[j].astype(BF16)
            lamv = jnp.stack([lambda_q1[j], lambda_k1[j], lambda_q2[j], lambda_k2[j]])
            sg = subln_g[j][None]
            q_scale = dh_c ** -0.5 * LOG2E

            x_in = xp if xp_b is None else xp_b
            (q_b,) = _mm(x_in, w_qkv, [BF16], scale=q_scale, col0=0, n=d_attn, tn_pref=2048)
            k_f, k_b = _mm(x_in, w_qkv, [F32, BF16], col0=d_attn, n=d_attn, tn_pref=2048)
            v_f, v_b = _mm(x_in, w_qkv, [F32, BF16], col0=2 * d_attn, n=d_attn, tn_pref=2048)
            att = _attn_prompt(q_b, k_b, v_b, slopes, lamv, sg, bp, t, nh_c, dh_c, dv_c, lam_init)
            xp, xp_b = _mm_ln(att, w_o, xp, g_mix, b_mix, alpha)
            pk.append(k_f.reshape(bp, t, nh_c, 2 * dh_c)); pv.append(v_f.reshape(bp, t, nh_c, dv_c))

            x_in = xs if xs_b is None else xs_b
            (q_s,) = _mm(x_in, w_qkv, [F32], scale=q_scale, col0=0, n=d_attn)
            (k_s,) = _mm(x_in, w_qkv, [F32], col0=d_attn, n=d_attn)
            (v_s,) = _mm(x_in, w_qkv, [F32], col0=2 * d_attn, n=d_attn)
            att = _attn_step(q_s.reshape(bs, nh_c, 2 * dh_c), k_s.reshape(bs, nh_c, 2 * dh_c),
                             v_s.reshape(bs, nh_c, dv_c), cache_k, cache_v, j, page_table,
                             slopes_rep, lamv, sg, nh_c, dh_c, dv_c, lam_init)
            xs, xs_b = _mm_ln(att.reshape(bs, d_attn).astype(BF16), w_o, xs, g_mix, b_mix, alpha)
            sk.append(k_s.reshape(bs, 1, nh_c, 2 * dh_c)); sv.append(v_s.reshape(bs, 1, nh_c, dv_c))

        xp, xp_b = _ffn(xp_b, wg_all, wu_all, wd_all, layer, xp, g_ffn, b_ffn, alpha)
        xs, xs_b = _ffn(xs_b, wg_all, wu_all, wd_all, layer, xs, g_ffn, b_ffn, alpha)

    return (xp.reshape(bp, t, d), xs.reshape(bs, 1, d),
            jnp.stack(pc), jnp.stack(pn), jnp.stack(pm), jnp.stack(pconv), jnp.stack(pk), jnp.stack(pv),
            jnp.stack(sc), jnp.stack(sn), jnp.stack(sm), jnp.stack(sconv), jnp.stack(sk), jnp.stack(sv))
```

```python
import functools
import math

import jax
import jax.numpy as jnp
from jax import lax
from jax.experimental import pallas as pl
from jax.experimental.pallas import tpu as pltpu

F32 = jnp.float32
BF16 = jnp.bfloat16

LN_EPS = 1e-5
CONV_W = 3
LOG2E = math.log2(math.e)
V7X_VMEM_LIMIT_BYTES = 56 * 1024 * 1024
LANES = 128
SUBLANES = 8


def _params(*sem):
    return pltpu.CompilerParams(dimension_semantics=sem, vmem_limit_bytes=V7X_VMEM_LIMIT_BYTES)


def _layer_norm(y, g, b):
    mu = jnp.mean(y, axis=-1, keepdims=True)
    d = y - mu
    var = jnp.mean(d * d, axis=-1, keepdims=True)
    return d * lax.rsqrt(var + LN_EPS) * g + b


def _log_sigmoid(x):
    return jnp.minimum(x, 0.0) - jnp.log1p(jnp.exp(-jnp.abs(x)))


def _tile(n, pref):
    if n <= pref:
        return n
    t = pref
    while n % t:
        t //= 2
    return t


def _mm_kernel(x_ref, w_ref, *refs, scale, cast_x, n_out, has_side):
    refs = list(refs)
    ws_ref = refs.pop(0) if has_side else None
    out_refs = [refs.pop(0) for _ in range(n_out)]
    side_ref = refs.pop(0) if has_side else None
    first = pl.program_id(1) == 0
    if cast_x:
        xb_ref = refs.pop(0)

        @pl.when(first)
        def _():
            xb_ref[...] = x_ref[...].astype(BF16)

        x = xb_ref[...]
    else:
        x = x_ref[...]
    if has_side:
        @pl.when(first)
        def _():
            side_ref[...] = jnp.dot(x, ws_ref[...], preferred_element_type=F32)

    acc = jnp.dot(x, w_ref[...], preferred_element_type=F32)
    if scale is not None:
        acc = acc * scale
    for o_ref in out_refs:
        o_ref[...] = acc.astype(o_ref.dtype)


def _mm(x, w, out_dtypes, scale=None, col0=0, n=None, w_side=None, tm_pref=1024, tn_pref=1024):
    m, k = x.shape
    n = w.shape[1] if n is None else n
    tm, tn = _tile(m, tm_pref), _tile(n, tn_pref)
    assert col0 % tn == 0
    j0 = col0 // tn
    cast_x = x.dtype != BF16
    has_side = w_side is not None
    kern = functools.partial(_mm_kernel, scale=scale, cast_x=cast_x, n_out=len(out_dtypes), has_side=has_side)
    in_specs = [pl.BlockSpec((tm, k), lambda i, j: (i, 0)),
                pl.BlockSpec((k, tn), lambda i, j: (0, j + j0))]
    out_specs = [pl.BlockSpec((tm, tn), lambda i, j: (i, j)) for _ in out_dtypes]
    out_shape = [jax.ShapeDtypeStruct((m, n), dt) for dt in out_dtypes]
    operands = [x, w]
    if has_side:
        n_side = w_side.shape[1]
        in_specs.append(pl.BlockSpec((k, n_side), lambda i, j: (0, 0)))
        out_specs.append(pl.BlockSpec((tm, n_side), lambda i, j: (i, 0)))
        out_shape.append(jax.ShapeDtypeStruct((m, n_side), F32))
        operands.append(w_side)
    outs = pl.pallas_call(
        kern,
        grid=(m // tm, n // tn),
        in_specs=in_specs,
        out_specs=out_specs,
        out_shape=out_shape,
        scratch_shapes=[pltpu.VMEM((tm, k), BF16)] if cast_x else [],
        compiler_params=_params("parallel", "arbitrary"),
        name="mm",
    )(*operands)
    return outs


def _mm_ln_kernel(x_ref, w_ref, res_ref, g_ref, b_ref, of_ref, ob_ref, *, alpha, nsub):
    rows = x_ref.shape[0] // nsub
    for r in range(nsub):
        sl = slice(r * rows, (r + 1) * rows)
        f = jnp.dot(x_ref[sl, :], w_ref[...], preferred_element_type=F32)
        y = _layer_norm(alpha * res_ref[sl, :] + f, g_ref[...], b_ref[...])
        of_ref[sl, :] = y
        ob_ref[sl, :] = y.astype(BF16)


def _mm_ln(x, w, res, g, b, alpha, tm_pref=512):
    m, k = x.shape
    d = w.shape[1]
    tm = _tile(m, tm_pref)
    return pl.pallas_call(
        functools.partial(_mm_ln_kernel, alpha=alpha, nsub=2 if tm % 512 == 0 else 1),
        grid=(m // tm,),
        in_specs=[pl.BlockSpec((tm, k), lambda i: (i, 0)),
                  pl.BlockSpec((k, d), lambda i: (0, 0)),
                  pl.BlockSpec((tm, d), lambda i: (i, 0)),
                  pl.BlockSpec((1, d), lambda i: (0, 0)),
                  pl.BlockSpec((1, d), lambda i: (0, 0))],
        out_specs=[pl.BlockSpec((tm, d), lambda i: (i, 0)),
                   pl.BlockSpec((tm, d), lambda i: (i, 0))],
        out_shape=[jax.ShapeDtypeStruct((m, d), F32), jax.ShapeDtypeStruct((m, d), BF16)],
        compiler_params=_params("parallel"),
        name="mm_ln",
    )(x, w, res, g, b)


def _ffn_kernel(x_ref, wg_ref, wu_ref, wd_ref, res_ref, g_ref, b_ref, of_ref, ob_ref, acc_ref, *, alpha):
    f = pl.program_id(1)

    @pl.when(f == 0)
    def _():
        acc_ref[...] = jnp.zeros_like(acc_ref)

    x = x_ref[...]
    gate = jnp.dot(x, wg_ref[...], preferred_element_type=F32)
    up = jnp.dot(x, wu_ref[...], preferred_element_type=F32)
    h = (gate * jax.nn.sigmoid(gate) * up).astype(BF16)
    acc_ref[...] += jnp.dot(h, wd_ref[...], preferred_element_type=F32)

    @pl.when(f == pl.num_programs(1) - 1)
    def _():
        y = _layer_norm(alpha * res_ref[...] + acc_ref[...], g_ref[...], b_ref[...])
        of_ref[...] = y
        ob_ref[...] = y.astype(BF16)


def _ffn(x, wg, wu, wd, layer, res, g, b, alpha, tm_pref=512, tf_pref=512):
    m, d = x.shape
    ff = wg.shape[2]
    tm, tf = _tile(m, tm_pref), _tile(ff, tf_pref)
    return pl.pallas_call(
        functools.partial(_ffn_kernel, alpha=alpha),
        grid=(m // tm, ff // tf),
        in_specs=[pl.BlockSpec((tm, d), lambda i, f: (i, 0)),
                  pl.BlockSpec((None, d, tf), lambda i, f: (layer, 0, f)),
                  pl.BlockSpec((None, d, tf), lambda i, f: (layer, 0, f)),
                  pl.BlockSpec((None, tf, d), lambda i, f: (layer, f, 0)),
                  pl.BlockSpec((tm, d), lambda i, f: (i, 0)),
                  pl.BlockSpec((1, d), lambda i, f: (0, 0)),
                  pl.BlockSpec((1, d), lambda i, f: (0, 0))],
        out_specs=[pl.BlockSpec((tm, d), lambda i, f: (i, 0)),
                   pl.BlockSpec((tm, d), lambda i, f: (i, 0))],
        out_shape=[jax.ShapeDtypeStruct((m, d), F32), jax.ShapeDtypeStruct((m, d), BF16)],
        scratch_shapes=[pltpu.VMEM((tm, d), F32)],
        compiler_params=_params("parallel", "arbitrary"),
        name="ffn",
    )(x, wg, wu, wd, res, g, b)


def _evenmix_kernel(qkv_ref, rest_ref, gate_ref, bg_ref, cw_ref,
                    mix_ref, c_out, n_out, m_out, cv_out,
                    c_s, n_s, m_s, u_s, *, nh, dk, dv, dc, chunk):
    c = pl.program_id(1)
    last = pl.num_programs(1) - 1
    dm = nh * dk
    L = chunk

    @pl.when(c == 0)
    def _():
        c_s[...] = jnp.zeros_like(c_s)
        n_s[...] = jnp.zeros_like(n_s)
        m_s[...] = jnp.zeros_like(m_s)
        u_s[...] = jnp.zeros_like(u_s)

    g = gate_ref[...] + bg_ref[...]
    lane = lax.broadcasted_iota(jnp.int32, (L, LANES), 1)
    row = lax.broadcasted_iota(jnp.int32, (L, LANES), 0)
    bcum = _log_sigmoid(g)
    s = 1
    while s < L:
        bcum = bcum + jnp.where(row >= s, pltpu.roll(bcum, s, 0), 0.0)
        s *= 2
    z = jnp.where(lane < nh, g, bcum)
    zt = z.T

    tr = lax.broadcasted_iota(jnp.int32, (L, L), 0)
    tc = lax.broadcasted_iota(jnp.int32, (L, L), 1)
    causal = tr >= tc
    lane1 = lax.broadcasted_iota(jnp.int32, (1, LANES), 1)
    m_vec = jnp.zeros((1, LANES), F32)

    for h in range(nh):
        qc = qkv_ref[:, h * dk:(h + 1) * dk]
        kc = qkv_ref[:, dm + h * dk:dm + (h + 1) * dk] * (dk ** -0.5)
        vc = qkv_ref[:, 2 * dm + h * dv:2 * dm + (h + 1) * dv]
        ig_col = z[:, h:h + 1]
        b_col = z[:, nh + h:nh + h + 1]
        ig_row = zt[h:h + 1, :]
        b_row = zt[nh + h:nh + h + 1, :]
        m_prev = m_s[h][:, 0:1]
        cmat = c_s[h]
        n_row = n_s[h]

        dmat = jnp.where(causal, b_col - b_row + ig_row, -jnp.inf)
        inter = b_col + m_prev
        m_t = jnp.maximum(inter, jnp.max(dmat, axis=-1, keepdims=True))
        w_intra = jnp.exp(dmat - m_t)
        w_inter = jnp.exp(inter - m_t)
        sc = lax.dot_general(qc, kc, (((1,), (1,)), ((), ())), preferred_element_type=F32) * w_intra
        num = (w_inter * jnp.dot(qc, cmat.astype(BF16), preferred_element_type=F32)
               + jnp.dot(sc.astype(BF16), vc, preferred_element_type=F32))
        qn = jnp.sum(qc.astype(F32) * n_row, axis=-1, keepdims=True)
        den = w_inter * qn + jnp.sum(sc, axis=-1, keepdims=True)
        out = num * (1.0 / jnp.maximum(jnp.abs(den), jnp.exp(-m_t)))

        m_new = m_t[L - 1:L, :]
        b_last = b_col[L - 1:L, :]
        g_inter = jnp.exp(b_last + m_prev - m_new)
        g_s = jnp.exp(b_last - b_col + ig_col - m_new)
        kg = kc.astype(F32) * g_s
        c_s[h] = g_inter * cmat + jnp.dot(kg.T.astype(BF16), vc, preferred_element_type=F32)
        n_s[h] = g_inter * n_row + jnp.sum(kg, axis=0, keepdims=True)
        m_s[h] = jnp.broadcast_to(m_new, (1, LANES))
        m_vec = jnp.where(lane1 == h, m_new, m_vec)

        o_gate = rest_ref[:, h * dv:(h + 1) * dv]
        mix_ref[:, h * dv:(h + 1) * dv] = (jax.nn.sigmoid(o_gate) * out).astype(BF16)

    base = nh * dv
    gb = rest_ref[:, base:base + dc]
    u = rest_ref[:, base + dc:base + 2 * dc] * rest_ref[:, base + 2 * dc:base + 3 * dc]
    prev2 = u_s[0:1, :]
    prev1 = u_s[1:2, :]
    r1, r2 = pltpu.roll(u, 1, 0), pltpu.roll(u, 2, 0)
    row8 = lax.broadcasted_iota(jnp.int32, (SUBLANES, dc), 0)
    u1 = jnp.concatenate([jnp.where(row8 == 0, prev1, r1[:SUBLANES]), r1[SUBLANES:]], axis=0)
    u2 = jnp.concatenate([jnp.where(row8 == 0, prev2, jnp.where(row8 == 1, prev1, r2[:SUBLANES])),
                          r2[SUBLANES:]], axis=0)
    yc = gb * (cw_ref[0:1, :] * u2 + cw_ref[1:2, :] * u1 + cw_ref[2:3, :] * u)
    mix_ref[:, dm:dm + dc] = yc.astype(BF16)
    u_s[0:1, :] = u[L - 2:L - 1, :]
    u_s[1:2, :] = u[L - 1:L, :]

    @pl.when(c == last)
    def _():
        c_out[0] = c_s[...]
        n_out[0] = n_s[...]
        m_out[0] = m_vec
        cv_out[0, 0:1, :] = u[L - 2:L - 1, :]
        cv_out[0, 1:2, :] = u[L - 1:L, :]


def _evenmix_prompt(qkv, rest, gates, b_gate_pad, conv_w, bsz, t, nh, dk, dv, dc, chunk):
    nc = t // chunk
    dm = nh * dk
    kern = functools.partial(_evenmix_kernel, nh=nh, dk=dk, dv=dv, dc=dc, chunk=chunk)
    return pl.pallas_call(
        kern,
        grid=(bsz, nc),
        in_specs=[pl.BlockSpec((chunk, 3 * dm), lambda b, c: (b * nc + c, 0)),
                  pl.BlockSpec((chunk, dm + 3 * dc), lambda b, c: (b * nc + c, 0)),
                  pl.BlockSpec((chunk, LANES), lambda b, c: (b * nc + c, 0)),
                  pl.BlockSpec((1, LANES), lambda b, c: (0, 0)),
                  pl.BlockSpec((CONV_W, dc), lambda b, c: (0, 0))],
        out_specs=[pl.BlockSpec((chunk, dm + dc), lambda b, c: (b * nc + c, 0)),
                   pl.BlockSpec((1, nh, dk, dv), lambda b, c: (b, 0, 0, 0)),
                   pl.BlockSpec((1, nh, 1, dk), lambda b, c: (b, 0, 0, 0)),
                   pl.BlockSpec((1, 1, LANES), lambda b, c: (b, 0, 0)),
                   pl.BlockSpec((1, CONV_W - 1, dc), lambda b, c: (b, 0, 0))],
        out_shape=[jax.ShapeDtypeStruct((bsz * t, dm + dc), BF16),
                   jax.ShapeDtypeStruct((bsz, nh, dk, dv), F32),
                   jax.ShapeDtypeStruct((bsz, nh, 1, dk), F32),
                   jax.ShapeDtypeStruct((bsz, 1, LANES), F32),
                   jax.ShapeDtypeStruct((bsz, CONV_W - 1, dc), F32)],
        scratch_shapes=[pltpu.VMEM((nh, dk, dv), F32),
                        pltpu.VMEM((nh, 1, dk), F32),
                        pltpu.VMEM((nh, 1, LANES), F32),
                        pltpu.VMEM((SUBLANES, dc), F32)],
        compiler_params=_params("parallel", "arbitrary"),
        name="evenmix_prompt",
    )(qkv, rest, gates, b_gate_pad, conv_w)


def _evenmix_step_kernel(q_ref, k_ref, v_ref, o_ref, conv_ref, gate_ref, bg_ref, cw_ref,
                         c_ref, n_ref, m_ref, cv_ref,
                         mix_ref, c_out, n_out, m_out, cv_out, *, nh, dk, dv, dc, nseq):
    dm = nh * dk
    scale = dk ** -0.5
    lane1 = lax.broadcasted_iota(jnp.int32, (1, LANES), 1)
    row128 = lax.broadcasted_iota(jnp.int32, (LANES, dk), 0)
    for b in range(nseq):
        q_rows = q_ref[b]
        k_rows = k_ref[b] * scale
        tile = jnp.zeros((LANES, dk), F32)
        for h in range(nh):
            tile = jnp.where(row128 == h, q_rows[h:h + 1, :], tile)
            tile = jnp.where(row128 == nh + h, k_rows[h:h + 1, :], tile)
        cols = tile.T
        g = gate_ref[b] + bg_ref[...]
        lf_all = _log_sigmoid(g)
        m_all = m_ref[b]
        m_vec = jnp.zeros((1, LANES), F32)
        for h in range(nh):
            q_col = cols[:, h:h + 1]
            k_col = cols[:, nh + h:nh + h + 1]
            q_row = q_rows[h:h + 1, :]
            k_row = k_rows[h:h + 1, :]
            v_row = v_ref[b, h:h + 1, :]
            ig = g[:, h:h + 1]
            lf = lf_all[:, nh + h:nh + h + 1]
            m_prev = m_all[:, h:h + 1]
            cmat = c_ref[b, h]
            n_row = n_ref[b, h]

            inter = lf + m_prev
            m_t = jnp.maximum(inter, ig)
            w_intra = jnp.exp(ig - m_t)
            w_inter = jnp.exp(inter - m_t)
            sc = jnp.sum(q_row * k_row, axis=-1, keepdims=True) * w_intra
            num = w_inter * jnp.sum(q_col * cmat, axis=0, keepdims=True) + sc * v_row
            den = w_inter * jnp.sum(q_row * n_row, axis=-1, keepdims=True) + sc
            out = num / jnp.maximum(jnp.abs(den), jnp.exp(-m_t))

            g_inter = jnp.exp(lf + m_prev - m_t)
            g_s = jnp.exp(ig - m_t)
            c_out[b, h] = g_inter * cmat + (g_s * k_col) * v_row
            n_out[b, h] = g_inter * n_row + g_s * k_row
            m_vec = jnp.where(lane1 == h, m_t, m_vec)
            mix_ref[b, :, h * dv:(h + 1) * dv] = jax.nn.sigmoid(o_ref[b, h:h + 1, :]) * out
        m_out[b] = m_vec

        gb = conv_ref[b, 0:1, :]
        u = conv_ref[b, 1:2, :] * conv_ref[b, 2:3, :]
        prev2 = cv_ref[b, 0:1, :]
        prev1 = cv_ref[b, 1:2, :]
        yc = gb * (cw_ref[0:1, :] * prev2 + cw_ref[1:2, :] * prev1 + cw_ref[2:3, :] * u)
        mix_ref[b, :, dm:dm + dc] = yc
        cv_out[b, 0:1, :] = prev1
        cv_out[b, 1:2, :] = u


def _evenmix_step(q, k, v, o, conv_in, gates, b_gate_pad, conv_w, c0, n0, m0_pad, cv0, nh, dk, dv, dc):
    bsz = q.shape[0]
    dm = nh * dk
    assert 2 * nh <= LANES
    nseq = _tile(bsz, 4)
    head_spec = lambda w: pl.BlockSpec((nseq, nh, w), lambda b: (b, 0, 0))
    kern = functools.partial(_evenmix_step_kernel, nh=nh, dk=dk, dv=dv, dc=dc, nseq=nseq)
    return pl.pallas_call(
        kern,
        grid=(bsz // nseq,),
        in_specs=[head_spec(dk), head_spec(dk), head_spec(dv), head_spec(dv),
                  pl.BlockSpec((nseq, 3, dc), lambda b: (b, 0, 0)),
                  pl.BlockSpec((nseq, 1, LANES), lambda b: (b, 0, 0)),
                  pl.BlockSpec((1, LANES), lambda b: (0, 0)),
                  pl.BlockSpec((CONV_W, dc), lambda b: (0, 0)),
                  pl.BlockSpec((nseq, nh, dk, dv), lambda b: (b, 0, 0, 0)),
                  pl.BlockSpec((nseq, nh, 1, dk), lambda b: (b, 0, 0, 0)),
                  pl.BlockSpec((nseq, 1, LANES), lambda b: (b, 0, 0)),
                  pl.BlockSpec((nseq, CONV_W - 1, dc), lambda b: (b, 0, 0))],
        out_specs=[pl.BlockSpec((nseq, 1, dm + dc), lambda b: (b, 0, 0)),
                   pl.BlockSpec((nseq, nh, dk, dv), lambda b: (b, 0, 0, 0)),
                   pl.BlockSpec((nseq, nh, 1, dk), lambda b: (b, 0, 0, 0)),
                   pl.BlockSpec((nseq, 1, LANES), lambda b: (b, 0, 0)),
                   pl.BlockSpec((nseq, CONV_W - 1, dc), lambda b: (b, 0, 0))],
        out_shape=[jax.ShapeDtypeStruct((bsz, 1, dm + dc), F32),
                   jax.ShapeDtypeStruct((bsz, nh, dk, dv), F32),
                   jax.ShapeDtypeStruct((bsz, nh, 1, dk), F32),
                   jax.ShapeDtypeStruct((bsz, 1, LANES), F32),
                   jax.ShapeDtypeStruct((bsz, CONV_W - 1, dc), F32)],
        compiler_params=_params("parallel"),
        name="evenmix_step",
    )(q.reshape(bsz, nh, dk), k.reshape(bsz, nh, dk), v.reshape(bsz, nh, dv), o.reshape(bsz, nh, dv),
      conv_in, gates, b_gate_pad, conv_w, c0, n0, m0_pad, cv0)


def _diff_lambda(lamv_ref, lam_init):
    a = jnp.sum(lamv_ref[0:1, :] * lamv_ref[1:2, :], axis=-1, keepdims=True)
    b = jnp.sum(lamv_ref[2:3, :] * lamv_ref[3:4, :], axis=-1, keepdims=True)
    return jnp.exp(a) - jnp.exp(b) + lam_init


def _attn_prompt_kernel(slope_ref, q_ref, k_ref, v_ref, lamv_ref, sg_ref, o_ref, kb_s,
                        *, dh, dv, blk, lam_init):
    h = pl.program_id(1)
    t_all = k_ref.shape[0]
    nq = t_all // blk

    pos = (lax.broadcasted_iota(jnp.int32, (t_all, LANES), 0).astype(F32) * (slope_ref[h] * LOG2E))
    lane = lax.broadcasted_iota(jnp.int32, (t_all, LANES), 1)
    hi = pos.astype(BF16).astype(F32)
    r1 = pos - hi
    mid = r1.astype(BF16).astype(F32)
    lo = r1 - mid
    pieces = jnp.where(lane == 0, hi, jnp.where(lane == 1, mid, jnp.where(lane == 2, lo, 0.0)))
    kb_s[...] = pieces.astype(BF16)

    half = blk // 2
    ones_cols = {n: jnp.where(lax.broadcasted_iota(jnp.int32, (n, LANES), 1) < 3, 1.0, 0.0).astype(BF16)
                 for n in (half, blk)}
    tri_top = (lax.broadcasted_iota(jnp.int32, (half, half), 0)
               >= lax.broadcasted_iota(jnp.int32, (half, half), 1))
    tri_bot = (lax.broadcasted_iota(jnp.int32, (half, blk), 0) + half
               >= lax.broadcasted_iota(jnp.int32, (half, blk), 1))
    lam = _diff_lambda(lamv_ref, lam_init)

    def scores(qs, ks):
        kb = kb_s[ks, :]
        out = []
        for c in range(2):
            qa = jnp.concatenate([q_ref[qs, c * dh:(c + 1) * dh], ones_cols[qs.stop - qs.start]], axis=1)
            ka = jnp.concatenate([k_ref[ks, c * dh:(c + 1) * dh].astype(BF16), kb], axis=1)
            out.append(lax.dot_general(qa, ka, (((1,), (1,)), ((), ())), preferred_element_type=F32))
        return out

    def update(state, s, v):
        nrow = s.shape[0]
        slabs = [s[:, j * LANES:(j + 1) * LANES] for j in range(s.shape[1] // LANES)]
        m_cur = jnp.max(functools.reduce(jnp.maximum, slabs), axis=-1, keepdims=True)
        m_new = jnp.broadcast_to(m_cur, (nrow, LANES)) if state is None else jnp.maximum(state[0], m_cur)
        ps = [jnp.exp2(sl - m_new) for sl in slabs]
        psum = jnp.sum(functools.reduce(lambda a, b: a + b, ps), axis=-1, keepdims=True)
        pv = jnp.dot(jnp.concatenate([x.astype(BF16) for x in ps], axis=1), v, preferred_element_type=F32)
        if state is None:
            return m_new, jnp.broadcast_to(psum, (nrow, LANES)), pv
        alpha = jnp.exp2(state[0] - m_new)
        acc = jnp.concatenate([alpha * state[2][:, e * LANES:(e + 1) * LANES] for e in range(dv // LANES)],
                              axis=1) + pv
        return m_new, alpha * state[1] + psum, acc

    for qi in range(nq):
        q0 = qi * blk
        rows = slice(q0, q0 + blk)
        state = [None, None]
        s_next = scores(rows, slice(0, blk)) if qi > 0 else None
        for ki in range(qi):
            s_cur = s_next
            s_next = scores(rows, slice((ki + 1) * blk, (ki + 2) * blk)) if ki + 1 < qi else None
            v = v_ref[ki * blk:(ki + 1) * blk, :].astype(BF16)
            for c in range(2):
                state[c] = update(state[c], s_cur[c], v)
        s_top = scores(slice(q0, q0 + half), slice(q0, q0 + half))
        s_bot = scores(slice(q0 + half, q0 + blk), rows)
        v_d = v_ref[rows, :].astype(BF16)
        for c in range(2):
            st = state[c]
            st_top = None if st is None else tuple(x[:half] for x in st)
            st_bot = None if st is None else tuple(x[half:] for x in st)
            st_top = update(st_top, jnp.where(tri_top, s_top[c], -jnp.inf), v_d[:half])
            st_bot = update(st_bot, jnp.where(tri_bot, s_bot[c], -jnp.inf), v_d)
            state[c] = tuple(jnp.concatenate([a, b], axis=0) for a, b in zip(st_top, st_bot))
        inv0, inv1 = 1.0 / state[0][1], lam / state[1][1]
        o = jnp.concatenate([state[0][2][:, e * LANES:(e + 1) * LANES] * inv0
                             - state[1][2][:, e * LANES:(e + 1) * LANES] * inv1
                             for e in range(dv // LANES)], axis=1)
        o = o * lax.rsqrt(jnp.mean(o * o, axis=-1, keepdims=True) + LN_EPS) * sg_ref[...]
        o_ref[qi * blk:(qi + 1) * blk, :] = (o * (1.0 - lam_init)).astype(BF16)


def _attn_prompt(q, k, v, slopes, lamv, subln_g, bsz, t, nh, dh, dv, lam_init, blk_pref=512):
    blk = _tile(t, blk_pref)
    kern = functools.partial(_attn_prompt_kernel, dh=dh, dv=dv, blk=blk, lam_init=lam_init)
    return pl.pallas_call(
        kern,
        grid=(bsz, nh),
        in_specs=[pl.BlockSpec(memory_space=pltpu.SMEM),
                  pl.BlockSpec((t, 2 * dh), lambda b, h: (b, h)),
                  pl.BlockSpec((t, 2 * dh), lambda b, h: (b, h)),
                  pl.BlockSpec((t, dv), lambda b, h: (b, h)),
                  pl.BlockSpec((4, dh), lambda b, h: (0, 0)),
                  pl.BlockSpec((1, dv), lambda b, h: (0, 0))],
        out_specs=pl.BlockSpec((t, dv), lambda b, h: (b, h)),
        out_shape=jax.ShapeDtypeStruct((bsz * t, nh * dv), BF16),
        scratch_shapes=[pltpu.VMEM((t, LANES), BF16)],
        compiler_params=_params("parallel", "parallel"),
        name="attn_prompt",
    )(slopes, q, k, v, lamv, subln_g)


def _attn_step_kernel(pt_ref, q_ref, kn_ref, vn_ref, slope_ref, lamv_ref, sg_ref, *refs,
                      dh, dv, page, pages_per_step, past, lam_init):
    del pt_ref
    npg = pages_per_step
    k_refs = refs[:npg]
    v_refs = refs[npg:2 * npg]
    o_ref = refs[2 * npg]
    m_s, l_s, acc_s, pos_s, q_s = refs[2 * npg + 1:]
    pp = pl.program_id(1)
    nh = q_ref.shape[1]
    nr, nl = 2 * nh, page * nh
    slope2 = jnp.concatenate([slope_ref[...], slope_ref[...]], axis=0)[:, 0:1]

    @pl.when(pp == 0)
    def _():
        q, kn, vn = q_ref[0], kn_ref[0], vn_ref[0]
        zeros = jnp.zeros((nh, dh), F32)
        q_s[...] = jnp.concatenate([jnp.concatenate([q[:, :dh], zeros], axis=1),
                                    jnp.concatenate([zeros, q[:, dh:]], axis=1)], axis=0)
        lane = lax.broadcasted_iota(jnp.int32, (nr, nl), 1)
        row = lax.broadcasted_iota(jnp.int32, (nr, nl), 0)
        same_head = jnp.bitwise_and(lane, nh - 1) == jnp.bitwise_and(row, nh - 1)
        token = lax.shift_right_logical(lane, nh.bit_length() - 1).astype(F32)
        pos_s[...] = jnp.where(same_head, slope2 * token, -jnp.inf)
        prod = q * kn
        m_s[...] = jnp.concatenate([jnp.sum(prod[:, :dh], axis=-1, keepdims=True),
                                    jnp.sum(prod[:, dh:], axis=-1, keepdims=True)], axis=0)
        l_s[...] = jnp.ones_like(l_s)
        acc_s[...] = jnp.concatenate([vn, vn], axis=0)

    qb = q_s[...]
    scores = []
    mx = None
    for i in range(npg):
        first_pos = (pp * npg + i) * page - past
        s = lax.dot_general(qb, k_refs[i][...].reshape(nl, 2 * dh), (((1,), (1,)), ((), ())),
                            preferred_element_type=F32)
        s = s + pos_s[...] + slope2 * first_pos.astype(F32)
        scores.append(s)
        cur = jnp.max(s, axis=-1, keepdims=True)
        mx = cur if mx is None else jnp.maximum(mx, cur)
    m_prev = m_s[...]
    m_new = jnp.maximum(m_prev, mx)
    alpha = jnp.exp2(m_prev - m_new)
    lsum = jnp.zeros_like(m_new)
    pv = jnp.zeros((nr, dv), F32)
    for i in range(npg):
        p = jnp.exp2(scores[i] - m_new)
        lsum = lsum + jnp.sum(p, axis=-1, keepdims=True)
        pv = pv + jnp.dot(p, v_refs[i][...].reshape(nl, dv), preferred_element_type=F32)
    l_s[...] = alpha * l_s[...] + lsum
    acc_s[...] = alpha * acc_s[...] + pv
    m_s[...] = m_new

    @pl.when(pp == pl.num_programs(1) - 1)
    def _():
        lam = _diff_lambda(lamv_ref, lam_init)
        a, l = acc_s[...], l_s[...]
        o = a[:nh] / l[:nh] - lam * (a[nh:] / l[nh:])
        o = o * lax.rsqrt(jnp.mean(o * o, axis=-1, keepdims=True) + LN_EPS) * sg_ref[...]
        o_ref[0] = o * (1.0 - lam_init)


def _attn_step(q, kn, vn, cache_k, cache_v, layer_j, page_table, slopes_rep, lamv, subln_g,
               nh, dh, dv, lam_init, pages_per_step=8):
    bsz, n_pages = page_table.shape
    page = cache_k.shape[2]
    past = n_pages * page
    assert nh & (nh - 1) == 0, "the (token, head) lane split uses bit masks"
    pps = pages_per_step if n_pages % pages_per_step == 0 else 1
    kern = functools.partial(_attn_step_kernel, dh=dh, dv=dv, page=page, pages_per_step=pps,
                             past=past, lam_init=lam_init)

    def page_spec(i, width):
        return pl.BlockSpec((None, None, page, nh, width),
                            lambda b, pp, pt: (layer_j, pt[b * n_pages + pp * pps + i], 0, 0, 0))

    tok_spec = lambda width: pl.BlockSpec((1, nh, width), lambda b, pp, pt: (b, 0, 0))
    grid_spec = pltpu.PrefetchScalarGridSpec(
        num_scalar_prefetch=1,
        grid=(bsz, n_pages // pps),
        in_specs=[tok_spec(2 * dh), tok_spec(2 * dh), tok_spec(dv),
                  pl.BlockSpec((nh, LANES), lambda b, pp, pt: (0, 0)),
                  pl.BlockSpec((4, dh), lambda b, pp, pt: (0, 0)),
                  pl.BlockSpec((1, dv), lambda b, pp, pt: (0, 0))]
                 + [page_spec(i, 2 * dh) for i in range(pps)]
                 + [page_spec(i, dv) for i in range(pps)],
        out_specs=tok_spec(dv),
        scratch_shapes=[pltpu.VMEM((2 * nh, 1), F32),
                        pltpu.VMEM((2 * nh, 1), F32),
                        pltpu.VMEM((2 * nh, dv), F32),
                        pltpu.VMEM((2 * nh, page * nh), F32),
                        pltpu.VMEM((2 * nh, 2 * dh), F32)],
    )
    return pl.pallas_call(
        kern,
        grid_spec=grid_spec,
        out_shape=jax.ShapeDtypeStruct((bsz, nh, dv), F32),
        compiler_params=_params("parallel", "arbitrary"),
        name="attn_step",
    )(page_table.reshape(-1), q, kn, vn, slopes_rep, lamv, subln_g,
      *([cache_k] * pps), *([cache_v] * pps))


def kernel(x_prompt, x_sample, state_mlstm_c, state_mlstm_n, state_mlstm_m, state_conv, cache_k, cache_v, page_table, w_in_even, b_gate_even, conv_w_even, w_out_even, w_qkv_odd, w_o_odd, lambda_q1, lambda_k1, lambda_q2, lambda_k2, subln_g, w_ffn_gate, w_ffn_up, w_ffn_down, ln_mix_g, ln_mix_b, ln_ffn_g, ln_ffn_b):
    bp, t, d = x_prompt.shape
    bs = x_sample.shape[0]
    assert x_sample.shape[1] == 1, "the sample group decodes one token per sequence"
    depth = w_ffn_gate.shape[0]
    nh_a, dk_a, dv_a = state_mlstm_c.shape[2:]
    dm = nh_a * dk_a
    dc = state_conv.shape[-1]
    nh_c, dv_c = cache_v.shape[3:]
    dh_c = cache_k.shape[4] // 2
    alpha = (2 * depth) ** 0.25
    chunk = math.gcd(t, 256)

    xp = x_prompt.reshape(bp * t, d)
    xs = x_sample.reshape(bs, d)
    xp_b = xs_b = None
    slopes = jnp.exp2(-8.0 * jnp.arange(1, nh_c + 1, dtype=F32) / nh_c)
    slopes_rep = jnp.broadcast_to((slopes * LOG2E)[:, None], (nh_c, LANES))

    wg_all, wu_all, wd_all = w_ffn_gate.astype(BF16), w_ffn_up.astype(BF16), w_ffn_down.astype(BF16)

    pc, pn, pm, pconv, pk, pv = [], [], [], [], [], []
    sc, sn, sm, sconv, sk, sv = [], [], [], [], [], []
    for layer in range(depth):
        j = layer // 2
        g_mix, b_mix = ln_mix_g[layer][None], ln_mix_b[layer][None]
        g_ffn, b_ffn = ln_ffn_g[layer][None], ln_ffn_b[layer][None]
        if layer % 2 == 0:
            w_in = w_in_even[j]
            w_qkv = w_in[:, :3 * dm].astype(BF16)
            gate_lo = 4 * dm
            w_gate = jnp.pad(w_in[:, gate_lo:gate_lo + 2 * nh_a], ((0, 0), (0, LANES - 2 * nh_a))).astype(BF16)
            w_rest = jnp.concatenate([w_in[:, 3 * dm:4 * dm], w_in[:, gate_lo + 2 * nh_a:]], axis=1).astype(BF16)
            w_out = w_out_even[j].astype(BF16)
            b_gate_pad = jnp.pad(b_gate_even[j], (0, LANES - 2 * nh_a))[None]
            conv_w = conv_w_even[j]

            x_in = xp if xp_b is None else xp_b
            (qkv,) = _mm(x_in, w_qkv, [BF16], tn_pref=3 * dm // 2)
            rest, gates = _mm(x_in, w_rest, [F32], w_side=w_gate)
            mix, c1, n1, m1, cv1 = _evenmix_prompt(qkv, rest, gates, b_gate_pad, conv_w,
                                                   bp, t, nh_a, dk_a, dv_a, dc, chunk)
            xp, xp_b = _mm_ln(mix, w_out, xp, g_mix, b_mix, alpha)
            pc.append(c1); pn.append(n1.reshape(bp, nh_a, dk_a)); pm.append(m1[:, 0, :nh_a]); pconv.append(cv1)

            x_in = xs if xs_b is None else xs_b
            (qkv,) = _mm(x_in, w_qkv, [F32])
            rest, gates = _mm(x_in, w_rest, [F32], w_side=w_gate)
            conv_in = rest[:, dm:].reshape(bs, 3, dc)
            m0_pad = jnp.pad(state_mlstm_m[j], ((0, 0), (0, LANES - nh_a)))[:, None, :]
            mix, c2, n2, m2, cv2 = _evenmix_step(
                qkv[:, :dm], qkv[:, dm:2 * dm], qkv[:, 2 * dm:], rest[:, :dm], conv_in,
                gates[:, None, :], b_gate_pad, conv_w, state_mlstm_c[j],
                state_mlstm_n[j][:, :, None, :], m0_pad, state_conv[j], nh_a, dk_a, dv_a, dc)
            xs, xs_b = _mm_ln(mix.reshape(bs, dm + dc).astype(BF16), w_out, xs, g_mix, b_mix, alpha)
            sc.append(c2); sn.append(n2.reshape(bs, nh_a, dk_a)); sm.append(m2[:, 0, :nh_a]); sconv.append(cv2)
        else:
            lam_init = 0.8 - 0.6 * math.exp(-0.3 * layer)
            d_attn = nh_c * dv_c
            w_qkv = w_qkv_odd[j].astype(BF16)
            w_o = w_o_odd[j].astype(BF16)
            lamv = jnp.stack([lambda_q1[j], lambda_k1[j], lambda_q2[j], lambda_k2[j]])
            sg = subln_g[j][None]
            q_scale = dh_c ** -0.5 * LOG2E

            x_in = xp if xp_b is None else xp_b
            (q_b,) = _mm(x_in, w_qkv, [BF16], scale=q_scale, col0=0, n=d_attn, tn_pref=2048)
            (k_f,) = _mm(x_in, w_qkv, [F32], col0=d_attn, n=d_attn, tn_pref=2048)
            (v_f,) = _mm(x_in, w_qkv, [F32], col0=2 * d_attn, n=d_attn, tn_pref=2048)
            att = _attn_prompt(q_b, k_f, v_f, slopes, lamv, sg, bp, t, nh_c, dh_c, dv_c, lam_init)
            xp, xp_b = _mm_ln(att, w_o, xp, g_mix, b_mix, alpha)
            pk.append(k_f.reshape(bp, t, nh_c, 2 * dh_c)); pv.append(v_f.reshape(bp, t, nh_c, dv_c))

            x_in = xs if xs_b is None else xs_b
            (q_s,) = _mm(x_in, w_qkv, [F32], scale=q_scale, col0=0, n=d_attn)
            (k_s,) = _mm(x_in, w_qkv, [F32], col0=d_attn, n=d_attn)
            (v_s,) = _mm(x_in, w_qkv, [F32], col0=2 * d_attn, n=d_attn)
            att = _attn_step(q_s.reshape(bs, nh_c, 2 * dh_c), k_s.reshape(bs, nh_c, 2 * dh_c),
                             v_s.reshape(bs, nh_c, dv_c), cache_k, cache_v, j, page_table,
                             slopes_rep, lamv, sg, nh_c, dh_c, dv_c, lam_init)
            xs, xs_b = _mm_ln(att.reshape(bs, d_attn).astype(BF16), w_o, xs, g_mix, b_mix, alpha)
            sk.append(k_s.reshape(bs, 1, nh_c, 2 * dh_c)); sv.append(v_s.reshape(bs, 1, nh_c, dv_c))

        xp, xp_b = _ffn(xp_b, wg_all, wu_all, wd_all, layer, xp, g_ffn, b_ffn, alpha)
        xs, xs_b = _ffn(xs_b, wg_all, wu_all, wd_all, layer, xs, g_ffn, b_ffn, alpha)

    return (xp.reshape(bp, t, d), xs.reshape(bs, 1, d),
            jnp.stack(pc), jnp.stack(pn), jnp.stack(pm), jnp.stack(pconv), jnp.stack(pk), jnp.stack(pv),
            jnp.stack(sc), jnp.stack(sn), jnp.stack(sm), jnp.stack(sconv), jnp.stack(sk), jnp.stack(sv))
```

```python
import functools
import math

import jax
import jax.numpy as jnp
from jax import lax
from jax.experimental import pallas as pl
from jax.experimental.pallas import tpu as pltpu

F32 = jnp.float32
BF16 = jnp.bfloat16

LN_EPS = 1e-5
CONV_W = 3
LOG2E = math.log2(math.e)
V7X_VMEM_LIMIT_BYTES = 56 * 1024 * 1024
LANES = 128
SUBLANES = 8


def _params(*sem):
    return pltpu.CompilerParams(dimension_semantics=sem, vmem_limit_bytes=V7X_VMEM_LIMIT_BYTES)


def _layer_norm(y, g, b):
    mu = jnp.mean(y, axis=-1, keepdims=True)
    d = y - mu
    var = jnp.mean(d * d, axis=-1, keepdims=True)
    return d * lax.rsqrt(var + LN_EPS) * g + b


def _log_sigmoid(x):
    return jnp.minimum(x, 0.0) - jnp.log1p(jnp.exp(-jnp.abs(x)))


def _tile(n, pref):
    if n <= pref:
        return n
    t = pref
    while n % t:
        t //= 2
    return t


def _mm_kernel(x_ref, w_ref, *refs, scale, cast_x, n_out, has_side):
    refs = list(refs)
    ws_ref = refs.pop(0) if has_side else None
    out_refs = [refs.pop(0) for _ in range(n_out)]
    side_ref = refs.pop(0) if has_side else None
    first = pl.program_id(1) == 0
    if cast_x:
        xb_ref = refs.pop(0)

        @pl.when(first)
        def _():
            xb_ref[...] = x_ref[...].astype(BF16)

        x = xb_ref[...]
    else:
        x = x_ref[...]
    if has_side:
        @pl.when(first)
        def _():
            side_ref[...] = jnp.dot(x, ws_ref[...], preferred_element_type=F32)

    acc = jnp.dot(x, w_ref[...], preferred_element_type=F32)
    if scale is not None:
        acc = acc * scale
    for o_ref in out_refs:
        o_ref[...] = acc.astype(o_ref.dtype)


def _mm(x, w, out_dtypes, scale=None, col0=0, n=None, w_side=None, tm_pref=1024, tn_pref=1024):
    m, k = x.shape
    n = w.shape[1] if n is None else n
    tm, tn = _tile(m, tm_pref), _tile(n, tn_pref)
    assert col0 % tn == 0
    j0 = col0 // tn
    cast_x = x.dtype != BF16
    has_side = w_side is not None
    kern = functools.partial(_mm_kernel, scale=scale, cast_x=cast_x, n_out=len(out_dtypes), has_side=has_side)
    in_specs = [pl.BlockSpec((tm, k), lambda i, j: (i, 0)),
                pl.BlockSpec((k, tn), lambda i, j: (0, j + j0))]
    out_specs = [pl.BlockSpec((tm, tn), lambda i, j: (i, j)) for _ in out_dtypes]
    out_shape = [jax.ShapeDtypeStruct((m, n), dt) for dt in out_dtypes]
    operands = [x, w]
    if has_side:
        n_side = w_side.shape[1]
        in_specs.append(pl.BlockSpec((k, n_side), lambda i, j: (0, 0)))
        out_specs.append(pl.BlockSpec((tm, n_side), lambda i, j: (i, 0)))
        out_shape.append(jax.ShapeDtypeStruct((m, n_side), F32))
        operands.append(w_side)
    outs = pl.pallas_call(
        kern,
        grid=(m // tm, n // tn),
        in_specs=in_specs,
        out_specs=out_specs,
        out_shape=out_shape,
        scratch_shapes=[pltpu.VMEM((tm, k), BF16)] if cast_x else [],
        compiler_params=_params("parallel", "arbitrary"),
        name="mm",
    )(*operands)
    return outs


def _mm_ln_kernel(x_ref, w_ref, res_ref, g_ref, b_ref, of_ref, ob_ref, *, alpha, nsub):
    rows = x_ref.shape[0] // nsub
    for r in range(nsub):
        sl = slice(r * rows, (r + 1) * rows)
        f = jnp.dot(x_ref[sl, :], w_ref[...], preferred_element_type=F32)
        y = _layer_norm(alpha * res_ref[sl, :] + f, g_ref[...], b_ref[...])
        of_ref[sl, :] = y
        ob_ref[sl, :] = y.astype(BF16)


def _mm_ln(x, w, res, g, b, alpha, tm_pref=512):
    m, k = x.shape
    d = w.shape[1]
    tm = _tile(m, tm_pref)
    return pl.pallas_call(
        functools.partial(_mm_ln_kernel, alpha=alpha, nsub=2 if tm % 512 == 0 else 1),
        grid=(m // tm,),
        in_specs=[pl.BlockSpec((tm, k), lambda i: (i, 0)),
                  pl.BlockSpec((k, d), lambda i: (0, 0)),
                  pl.BlockSpec((tm, d), lambda i: (i, 0)),
                  pl.BlockSpec((1, d), lambda i: (0, 0)),
                  pl.BlockSpec((1, d), lambda i: (0, 0))],
        out_specs=[pl.BlockSpec((tm, d), lambda i: (i, 0)),
                   pl.BlockSpec((tm, d), lambda i: (i, 0))],
        out_shape=[jax.ShapeDtypeStruct((m, d), F32), jax.ShapeDtypeStruct((m, d), BF16)],
        compiler_params=_params("parallel"),
        name="mm_ln",
    )(x, w, res, g, b)


def _ffn_kernel(x_ref, wg_ref, wu_ref, wd_ref, res_ref, g_ref, b_ref, of_ref, ob_ref, acc_ref, *, alpha):
    f = pl.program_id(1)

    @pl.when(f == 0)
    def _():
        acc_ref[...] = jnp.zeros_like(acc_ref)

    x = x_ref[...]
    gate = jnp.dot(x, wg_ref[...], preferred_element_type=F32)
    up = jnp.dot(x, wu_ref[...], preferred_element_type=F32)
    h = (gate * jax.nn.sigmoid(gate) * up).astype(BF16)
    acc_ref[...] += jnp.dot(h, wd_ref[...], preferred_element_type=F32)

    @pl.when(f == pl.num_programs(1) - 1)
    def _():
        y = _layer_norm(alpha * res_ref[...] + acc_ref[...], g_ref[...], b_ref[...])
        of_ref[...] = y
        ob_ref[...] = y.astype(BF16)


def _ffn(x, wg, wu, wd, layer, res, g, b, alpha, tm_pref=512, tf_pref=512):
    m, d = x.shape
    ff = wg.shape[2]
    tm, tf = _tile(m, tm_pref), _tile(ff, tf_pref)
    return pl.pallas_call(
        functools.partial(_ffn_kernel, alpha=alpha),
        grid=(m // tm, ff // tf),
        in_specs=[pl.BlockSpec((tm, d), lambda i, f: (i, 0)),
                  pl.BlockSpec((None, d, tf), lambda i, f: (layer, 0, f)),
                  pl.BlockSpec((None, d, tf), lambda i, f: (layer, 0, f)),
                  pl.BlockSpec((None, tf, d), lambda i, f: (layer, f, 0)),
                  pl.BlockSpec((tm, d), lambda i, f: (i, 0)),
                  pl.BlockSpec((1, d), lambda i, f: (0, 0)),
                  pl.BlockSpec((1, d), lambda i, f: (0, 0))],
        out_specs=[pl.BlockSpec((tm, d), lambda i, f: (i, 0)),
                   pl.BlockSpec((tm, d), lambda i, f: (i, 0))],
        out_shape=[jax.ShapeDtypeStruct((m, d), F32), jax.ShapeDtypeStruct((m, d), BF16)],
        scratch_shapes=[pltpu.VMEM((tm, d), F32)],
        compiler_params=_params("parallel", "arbitrary"),
        name="ffn",
    )(x, wg, wu, wd, res, g, b)


def _evenmix_kernel(qkv_ref, rest_ref, gate_ref, bg_ref, cw_ref,
                    mix_ref, c_out, n_out, m_out, cv_out,
                    c_s, n_s, m_s, u_s, *, nh, dk, dv, dc, chunk):
    c = pl.program_id(1)
    last = pl.num_programs(1) - 1
    dm = nh * dk
    L = chunk

    @pl.when(c == 0)
    def _():
        c_s[...] = jnp.zeros_like(c_s)
        n_s[...] = jnp.zeros_like(n_s)
        m_s[...] = jnp.zeros_like(m_s)
        u_s[...] = jnp.zeros_like(u_s)

    g = gate_ref[...] + bg_ref[...]
    lane = lax.broadcasted_iota(jnp.int32, (L, LANES), 1)
    row = lax.broadcasted_iota(jnp.int32, (L, LANES), 0)
    bcum = _log_sigmoid(g)
    s = 1
    while s < L:
        bcum = bcum + jnp.where(row >= s, pltpu.roll(bcum, s, 0), 0.0)
        s *= 2
    z = jnp.where(lane < nh, g, bcum)
    zt = z.T

    tr = lax.broadcasted_iota(jnp.int32, (L, L), 0)
    tc = lax.broadcasted_iota(jnp.int32, (L, L), 1)
    causal = tr >= tc
    lane1 = lax.broadcasted_iota(jnp.int32, (1, LANES), 1)
    m_vec = jnp.zeros((1, LANES), F32)

    for h in range(nh):
        qc = qkv_ref[:, h * dk:(h + 1) * dk]
        kc = qkv_ref[:, dm + h * dk:dm + (h + 1) * dk] * (dk ** -0.5)
        vc = qkv_ref[:, 2 * dm + h * dv:2 * dm + (h + 1) * dv]
        ig_col = z[:, h:h + 1]
        b_col = z[:, nh + h:nh + h + 1]
        ig_row = zt[h:h + 1, :]
        b_row = zt[nh + h:nh + h + 1, :]
        m_prev = m_s[h][:, 0:1]
        cmat = c_s[h]
        n_row = n_s[h]

        dmat = jnp.where(causal, b_col - b_row + ig_row, -jnp.inf)
        inter = b_col + m_prev
        m_t = jnp.maximum(inter, jnp.max(dmat, axis=-1, keepdims=True))
        w_intra = jnp.exp(dmat - m_t)
        w_inter = jnp.exp(inter - m_t)
        sc = lax.dot_general(qc, kc, (((1,), (1,)), ((), ())), preferred_element_type=F32) * w_intra
        num = (w_inter * jnp.dot(qc, cmat.astype(BF16), preferred_element_type=F32)
               + jnp.dot(sc.astype(BF16), vc, preferred_element_type=F32))
        qn = jnp.sum(qc.astype(F32) * n_row, axis=-1, keepdims=True)
        den = w_inter * qn + jnp.sum(sc, axis=-1, keepdims=True)
        out = num * (1.0 / jnp.maximum(jnp.abs(den), jnp.exp(-m_t)))

        m_new = m_t[L - 1:L, :]
        b_last = b_col[L - 1:L, :]
        g_inter = jnp.exp(b_last + m_prev - m_new)
        g_s = jnp.exp(b_last - b_col + ig_col - m_new)
        kg = kc.astype(F32) * g_s
        c_s[h] = g_inter * cmat + jnp.dot(kg.T.astype(BF16), vc, preferred_element_type=F32)
        n_s[h] = g_inter * n_row + jnp.sum(kg, axis=0, keepdims=True)
        m_s[h] = jnp.broadcast_to(m_new, (1, LANES))
        m_vec = jnp.where(lane1 == h, m_new, m_vec)

        o_gate = rest_ref[:, h * dv:(h + 1) * dv]
        mix_ref[:, h * dv:(h + 1) * dv] = (jax.nn.sigmoid(o_gate) * out).astype(BF16)

    base = nh * dv
    gb = rest_ref[:, base:base + dc]
    u = rest_ref[:, base + dc:base + 2 * dc] * rest_ref[:, base + 2 * dc:base + 3 * dc]
    prev2 = u_s[0:1, :]
    prev1 = u_s[1:2, :]
    r1, r2 = pltpu.roll(u, 1, 0), pltpu.roll(u, 2, 0)
    row8 = lax.broadcasted_iota(jnp.int32, (SUBLANES, dc), 0)
    u1 = jnp.concatenate([jnp.where(row8 == 0, prev1, r1[:SUBLANES]), r1[SUBLANES:]], axis=0)
    u2 = jnp.concatenate([jnp.where(row8 == 0, prev2, jnp.where(row8 == 1, prev1, r2[:SUBLANES])),
                          r2[SUBLANES:]], axis=0)
    yc = gb * (cw_ref[0:1, :] * u2 + cw_ref[1:2, :] * u1 + cw_ref[2:3, :] * u)
    mix_ref[:, dm:dm + dc] = yc.astype(BF16)
    u_s[0:1, :] = u[L - 2:L - 1, :]
    u_s[1:2, :] = u[L - 1:L, :]

    @pl.when(c == last)
    def _():
        c_out[0] = c_s[...]
        n_out[0] = n_s[...]
        m_out[0] = m_vec
        cv_out[0, 0:1, :] = u[L - 2:L - 1, :]
        cv_out[0, 1:2, :] = u[L - 1:L, :]


def _evenmix_prompt(qkv, rest, gates, b_gate_pad, conv_w, bsz, t, nh, dk, dv, dc, chunk):
    nc = t // chunk
    dm = nh * dk
    kern = functools.partial(_evenmix_kernel, nh=nh, dk=dk, dv=dv, dc=dc, chunk=chunk)
    return pl.pallas_call(
        kern,
        grid=(bsz, nc),
        in_specs=[pl.BlockSpec((chunk, 3 * dm), lambda b, c: (b * nc + c, 0)),
                  pl.BlockSpec((chunk, dm + 3 * dc), lambda b, c: (b * nc + c, 0)),
                  pl.BlockSpec((chunk, LANES), lambda b, c: (b * nc + c, 0)),
                  pl.BlockSpec((1, LANES), lambda b, c: (0, 0)),
                  pl.BlockSpec((CONV_W, dc), lambda b, c: (0, 0))],
        out_specs=[pl.BlockSpec((chunk, dm + dc), lambda b, c: (b * nc + c, 0)),
                   pl.BlockSpec((1, nh, dk, dv), lambda b, c: (b, 0, 0, 0)),
                   pl.BlockSpec((1, nh, 1, dk), lambda b, c: (b, 0, 0, 0)),
                   pl.BlockSpec((1, 1, LANES), lambda b, c: (b, 0, 0)),
                   pl.BlockSpec((1, CONV_W - 1, dc), lambda b, c: (b, 0, 0))],
        out_shape=[jax.ShapeDtypeStruct((bsz * t, dm + dc), BF16),
                   jax.ShapeDtypeStruct((bsz, nh, dk, dv), F32),
                   jax.ShapeDtypeStruct((bsz, nh, 1, dk), F32),
                   jax.ShapeDtypeStruct((bsz, 1, LANES), F32),
                   jax.ShapeDtypeStruct((bsz, CONV_W - 1, dc), F32)],
        scratch_shapes=[pltpu.VMEM((nh, dk, dv), F32),
                        pltpu.VMEM((nh, 1, dk), F32),
                        pltpu.VMEM((nh, 1, LANES), F32),
                        pltpu.VMEM((SUBLANES, dc), F32)],
        compiler_params=_params("parallel", "arbitrary"),
        name="evenmix_prompt",
    )(qkv, rest, gates, b_gate_pad, conv_w)


def _evenmix_step_kernel(q_ref, k_ref, v_ref, o_ref, conv_ref, gate_ref, bg_ref, cw_ref,
                         c_ref, n_ref, m_ref, cv_ref,
                         mix_ref, c_out, n_out, m_out, cv_out, *, nh, dk, dv, dc, nseq):
    dm = nh * dk
    scale = dk ** -0.5
    lane1 = lax.broadcasted_iota(jnp.int32, (1, LANES), 1)
    row128 = lax.broadcasted_iota(jnp.int32, (LANES, dk), 0)
    for b in range(nseq):
        q_rows = q_ref[b]
        k_rows = k_ref[b] * scale
        tile = jnp.zeros((LANES, dk), F32)
        for h in range(nh):
            tile = jnp.where(row128 == h, q_rows[h:h + 1, :], tile)
            tile = jnp.where(row128 == nh + h, k_rows[h:h + 1, :], tile)
        cols = tile.T
        g = gate_ref[b] + bg_ref[...]
        lf_all = _log_sigmoid(g)
        m_all = m_ref[b]
        m_vec = jnp.zeros((1, LANES), F32)
        for h in range(nh):
            q_col = cols[:, h:h + 1]
            k_col = cols[:, nh + h:nh + h + 1]
            q_row = q_rows[h:h + 1, :]
            k_row = k_rows[h:h + 1, :]
            v_row = v_ref[b, h:h + 1, :]
            ig = g[:, h:h + 1]
            lf = lf_all[:, nh + h:nh + h + 1]
            m_prev = m_all[:, h:h + 1]
            cmat = c_ref[b, h]
            n_row = n_ref[b, h]

            inter = lf + m_prev
            m_t = jnp.maximum(inter, ig)
            w_intra = jnp.exp(ig - m_t)
            w_inter = jnp.exp(inter - m_t)
            sc = jnp.sum(q_row * k_row, axis=-1, keepdims=True) * w_intra
            num = w_inter * jnp.sum(q_col * cmat, axis=0, keepdims=True) + sc * v_row
            den = w_inter * jnp.sum(q_row * n_row, axis=-1, keepdims=True) + sc
            out = num / jnp.maximum(jnp.abs(den), jnp.exp(-m_t))

            g_inter = jnp.exp(lf + m_prev - m_t)
            g_s = jnp.exp(ig - m_t)
            c_out[b, h] = g_inter * cmat + (g_s * k_col) * v_row
            n_out[b, h] = g_inter * n_row + g_s * k_row
            m_vec = jnp.where(lane1 == h, m_t, m_vec)
            mix_ref[b, :, h * dv:(h + 1) * dv] = jax.nn.sigmoid(o_ref[b, h:h + 1, :]) * out
        m_out[b] = m_vec

        gb = conv_ref[b, 0:1, :]
        u = conv_ref[b, 1:2, :] * conv_ref[b, 2:3, :]
        prev2 = cv_ref[b, 0:1, :]
        prev1 = cv_ref[b, 1:2, :]
        yc = gb * (cw_ref[0:1, :] * prev2 + cw_ref[1:2, :] * prev1 + cw_ref[2:3, :] * u)
        mix_ref[b, :, dm:dm + dc] = yc
        cv_out[b, 0:1, :] = prev1
        cv_out[b, 1:2, :] = u


def _evenmix_step(q, k, v, o, conv_in, gates, b_gate_pad, conv_w, c0, n0, m0_pad, cv0, nh, dk, dv, dc):
    bsz = q.shape[0]
    dm = nh * dk
    assert 2 * nh <= LANES
    nseq = _tile(bsz, 4)
    head_spec = lambda w: pl.BlockSpec((nseq, nh, w), lambda b: (b, 0, 0))
    kern = functools.partial(_evenmix_step_kernel, nh=nh, dk=dk, dv=dv, dc=dc, nseq=nseq)
    return pl.pallas_call(
        kern,
        grid=(bsz // nseq,),
        in_specs=[head_spec(dk), head_spec(dk), head_spec(dv), head_spec(dv),
                  pl.BlockSpec((nseq, 3, dc), lambda b: (b, 0, 0)),
                  pl.BlockSpec((nseq, 1, LANES), lambda b: (b, 0, 0)),
                  pl.BlockSpec((1, LANES), lambda b: (0, 0)),
                  pl.BlockSpec((CONV_W, dc), lambda b: (0, 0)),
                  pl.BlockSpec((nseq, nh, dk, dv), lambda b: (b, 0, 0, 0)),
                  pl.BlockSpec((nseq, nh, 1, dk), lambda b: (b, 0, 0, 0)),
                  pl.BlockSpec((nseq, 1, LANES), lambda b: (b, 0, 0)),
                  pl.BlockSpec((nseq, CONV_W - 1, dc), lambda b: (b, 0, 0))],
        out_specs=[pl.BlockSpec((nseq, 1, dm + dc), lambda b: (b, 0, 0)),
                   pl.BlockSpec((nseq, nh, dk, dv), lambda b: (b, 0, 0, 0)),
                   pl.BlockSpec((nseq, nh, 1, dk), lambda b: (b, 0, 0, 0)),
                   pl.BlockSpec((nseq, 1, LANES), lambda b: (b, 0, 0)),
                   pl.BlockSpec((nseq, CONV_W - 1, dc), lambda b: (b, 0, 0))],
        out_shape=[jax.ShapeDtypeStruct((bsz, 1, dm + dc), F32),
                   jax.ShapeDtypeStruct((bsz, nh, dk, dv), F32),
                   jax.ShapeDtypeStruct((bsz, nh, 1, dk), F32),
                   jax.ShapeDtypeStruct((bsz, 1, LANES), F32),
                   jax.ShapeDtypeStruct((bsz, CONV_W - 1, dc), F32)],
        compiler_params=_params("parallel"),
        name="evenmix_step",
    )(q.reshape(bsz, nh, dk), k.reshape(bsz, nh, dk), v.reshape(bsz, nh, dv), o.reshape(bsz, nh, dv),
      conv_in, gates, b_gate_pad, conv_w, c0, n0, m0_pad, cv0)


def _diff_lambda(lamv_ref, lam_init):
    a = jnp.sum(lamv_ref[0:1, :] * lamv_ref[1:2, :], axis=-1, keepdims=True)
    b = jnp.sum(lamv_ref[2:3, :] * lamv_ref[3:4, :], axis=-1, keepdims=True)
    return jnp.exp(a) - jnp.exp(b) + lam_init


def _attn_prompt_kernel(slope_ref, q_ref, k_ref, v_ref, lamv_ref, sg_ref, o_ref, kb_s,
                        *, dh, dv, blk, lam_init):
    h = pl.program_id(1)
    t_all = k_ref.shape[0]
    nq = t_all // blk

    pos = (lax.broadcasted_iota(jnp.int32, (t_all, LANES), 0).astype(F32) * (slope_ref[h] * LOG2E))
    lane = lax.broadcasted_iota(jnp.int32, (t_all, LANES), 1)
    hi = pos.astype(BF16).astype(F32)
    r1 = pos - hi
    mid = r1.astype(BF16).astype(F32)
    lo = r1 - mid
    pieces = jnp.where(lane == 0, hi, jnp.where(lane == 1, mid, jnp.where(lane == 2, lo, 0.0)))
    kb_s[...] = pieces.astype(BF16)

    half = blk // 2
    ones_cols = {n: jnp.where(lax.broadcasted_iota(jnp.int32, (n, LANES), 1) < 3, 1.0, 0.0).astype(BF16)
                 for n in (half, blk)}
    tri_top = (lax.broadcasted_iota(jnp.int32, (half, half), 0)
               >= lax.broadcasted_iota(jnp.int32, (half, half), 1))
    tri_bot = (lax.broadcasted_iota(jnp.int32, (half, blk), 0) + half
               >= lax.broadcasted_iota(jnp.int32, (half, blk), 1))
    lam = _diff_lambda(lamv_ref, lam_init)

    def scores(qs, ks):
        kb = kb_s[ks, :]
        out = []
        for c in range(2):
            qa = jnp.concatenate([q_ref[qs, c * dh:(c + 1) * dh], ones_cols[qs.stop - qs.start]], axis=1)
            ka = jnp.concatenate([k_ref[ks, c * dh:(c + 1) * dh].astype(BF16), kb], axis=1)
            out.append(lax.dot_general(qa, ka, (((1,), (1,)), ((), ())), preferred_element_type=F32))
        return out

    def update(state, s, v):
        nrow = s.shape[0]
        slabs = [s[:, j * LANES:(j + 1) * LANES] for j in range(s.shape[1] // LANES)]
        m_cur = jnp.max(functools.reduce(jnp.maximum, slabs), axis=-1, keepdims=True)
        m_new = jnp.broadcast_to(m_cur, (nrow, LANES)) if state is None else jnp.maximum(state[0], m_cur)
        ps = [jnp.exp2(sl - m_new) for sl in slabs]
        psum = jnp.sum(functools.reduce(lambda a, b: a + b, ps), axis=-1, keepdims=True)
        pv = jnp.dot(jnp.concatenate([x.astype(BF16) for x in ps], axis=1), v, preferred_element_type=F32)
        if state is None:
            return m_new, jnp.broadcast_to(psum, (nrow, LANES)), pv
        alpha = jnp.exp2(state[0] - m_new)
        acc = jnp.concatenate([alpha * state[2][:, e * LANES:(e + 1) * LANES] for e in range(dv // LANES)],
                              axis=1) + pv
        return m_new, alpha * state[1] + psum, acc

    for qi in range(nq):
        q0 = qi * blk
        rows = slice(q0, q0 + blk)
        state = [None, None]
        s_next = scores(rows, slice(0, blk)) if qi > 0 else None
        for ki in range(qi):
            s_cur = s_next
            s_next = scores(rows, slice((ki + 1) * blk, (ki + 2) * blk)) if ki + 1 < qi else None
            v = v_ref[ki * blk:(ki + 1) * blk, :].astype(BF16)
            for c in range(2):
                state[c] = update(state[c], s_cur[c], v)
        s_top = scores(slice(q0, q0 + half), slice(q0, q0 + half))
        s_bot = scores(slice(q0 + half, q0 + blk), rows)
        v_d = v_ref[rows, :].astype(BF16)
        for c in range(2):
            st = state[c]
            st_top = None if st is None else tuple(x[:half] for x in st)
            st_bot = None if st is None else tuple(x[half:] for x in st)
            st_top = update(st_top, jnp.where(tri_top, s_top[c], -jnp.inf), v_d[:half])
            st_bot = update(st_bot, jnp.where(tri_bot, s_bot[c], -jnp.inf), v_d)
            state[c] = tuple(jnp.concatenate([a, b], axis=0) for a, b in zip(st_top, st_bot))
        inv0, inv1 = 1.0 / state[0][1], lam / state[1][1]
        o = jnp.concatenate([state[0][2][:, e * LANES:(e + 1) * LANES] * inv0
                             - state[1][2][:, e * LANES:(e + 1) * LANES] * inv1
                             for e in range(dv // LANES)], axis=1)
        o = o * lax.rsqrt(jnp.mean(o * o, axis=-1, keepdims=True) + LN_EPS) * sg_ref[...]
        o_ref[qi * blk:(qi + 1) * blk, :] = (o * (1.0 - lam_init)).astype(BF16)


def _attn_prompt(q, k, v, slopes, lamv, subln_g, bsz, t, nh, dh, dv, lam_init, blk_pref=512):
    blk = _tile(t, blk_pref)
    kern = functools.partial(_attn_prompt_kernel, dh=dh, dv=dv, blk=blk, lam_init=lam_init)
    return pl.pallas_call(
        kern,
        grid=(bsz, nh),
        in_specs=[pl.BlockSpec(memory_space=pltpu.SMEM),
                  pl.BlockSpec((t, 2 * dh), lambda b, h: (b, h)),
                  pl.BlockSpec((t, 2 * dh), lambda b, h: (b, h)),
                  pl.BlockSpec((t, dv), lambda b, h: (b, h)),
                  pl.BlockSpec((4, dh), lambda b, h: (0, 0)),
                  pl.BlockSpec((1, dv), lambda b, h: (0, 0))],
        out_specs=pl.BlockSpec((t, dv), lambda b, h: (b, h)),
        out_shape=jax.ShapeDtypeStruct((bsz * t, nh * dv), BF16),
        scratch_shapes=[pltpu.VMEM((t, LANES), BF16)],
        compiler_params=_params("parallel", "parallel"),
        name="attn_prompt",
    )(slopes, q, k, v, lamv, subln_g)


def _attn_step_kernel(pt_ref, q_ref, kn_ref, vn_ref, slope_ref, lamv_ref, sg_ref, *refs,
                      dh, dv, page, pages_per_step, past, lam_init, q_scale):
    del pt_ref
    npg = pages_per_step
    k_refs = refs[:npg]
    v_refs = refs[npg:2 * npg]
    o_ref = refs[2 * npg]
    m_s, l_s, acc_s, pos_s, q_s = refs[2 * npg + 1:]
    pp = pl.program_id(1)
    nh = q_ref.shape[1]
    nr, nl = 2 * nh, page * nh
    slope2 = jnp.concatenate([slope_ref[...], slope_ref[...]], axis=0)[:, 0:1]

    @pl.when(pp == 0)
    def _():
        q, kn, vn = q_ref[0] * q_scale, kn_ref[0], vn_ref[0]
        zeros = jnp.zeros((nh, dh), F32)
        q_s[...] = jnp.concatenate([jnp.concatenate([q[:, :dh], zeros], axis=1),
                                    jnp.concatenate([zeros, q[:, dh:]], axis=1)], axis=0)
        lane = lax.broadcasted_iota(jnp.int32, (nr, nl), 1)
        row = lax.broadcasted_iota(jnp.int32, (nr, nl), 0)
        same_head = jnp.bitwise_and(lane, nh - 1) == jnp.bitwise_and(row, nh - 1)
        token = lax.shift_right_logical(lane, nh.bit_length() - 1).astype(F32)
        pos_s[...] = jnp.where(same_head, slope2 * token, -jnp.inf)
        prod = q * kn
        m_s[...] = jnp.concatenate([jnp.sum(prod[:, :dh], axis=-1, keepdims=True),
                                    jnp.sum(prod[:, dh:], axis=-1, keepdims=True)], axis=0)
        l_s[...] = jnp.ones_like(l_s)
        acc_s[...] = jnp.concatenate([vn, vn], axis=0)

    qb = q_s[...]
    scores = []
    mx = None
    for i in range(npg):
        first_pos = (pp * npg + i) * page - past
        s = lax.dot_general(qb, k_refs[i][...].reshape(nl, 2 * dh), (((1,), (1,)), ((), ())),
                            preferred_element_type=F32)
        s = s + pos_s[...] + slope2 * first_pos.astype(F32)
        scores.append(s)
        cur = jnp.max(s, axis=-1, keepdims=True)
        mx = cur if mx is None else jnp.maximum(mx, cur)
    m_prev = m_s[...]
    m_new = jnp.maximum(m_prev, mx)
    alpha = jnp.exp2(m_prev - m_new)
    lsum = jnp.zeros_like(m_new)
    pv = jnp.zeros((nr, dv), F32)
    for i in range(npg):
        p = jnp.exp2(scores[i] - m_new)
        lsum = lsum + jnp.sum(p, axis=-1, keepdims=True)
        pv = pv + jnp.dot(p, v_refs[i][...].reshape(nl, dv), preferred_element_type=F32)
    l_s[...] = alpha * l_s[...] + lsum
    acc_s[...] = alpha * acc_s[...] + pv
    m_s[...] = m_new

    @pl.when(pp == pl.num_programs(1) - 1)
    def _():
        lam = _diff_lambda(lamv_ref, lam_init)
        a, l = acc_s[...], l_s[...]
        o = a[:nh] / l[:nh] - lam * (a[nh:] / l[nh:])
        o = o * lax.rsqrt(jnp.mean(o * o, axis=-1, keepdims=True) + LN_EPS) * sg_ref[...]
        o_ref[0] = o * (1.0 - lam_init)


def _attn_step(q, kn, vn, cache_k, cache_v, layer_j, page_table, slopes_rep, lamv, subln_g,
               nh, dh, dv, lam_init, q_scale, pages_per_step=8):
    bsz, n_pages = page_table.shape
    page = cache_k.shape[2]
    past = n_pages * page
    assert nh & (nh - 1) == 0, "the (token, head) lane split uses bit masks"
    pps = pages_per_step if n_pages % pages_per_step == 0 else 1
    kern = functools.partial(_attn_step_kernel, dh=dh, dv=dv, page=page, pages_per_step=pps,
                             past=past, lam_init=lam_init, q_scale=q_scale)

    def page_spec(i, width):
        return pl.BlockSpec((None, None, page, nh, width),
                            lambda b, pp, pt: (layer_j, pt[b * n_pages + pp * pps + i], 0, 0, 0))

    tok_spec = lambda width: pl.BlockSpec((1, nh, width), lambda b, pp, pt: (b, 0, 0))
    grid_spec = pltpu.PrefetchScalarGridSpec(
        num_scalar_prefetch=1,
        grid=(bsz, n_pages // pps),
        in_specs=[tok_spec(2 * dh), tok_spec(2 * dh), tok_spec(dv),
                  pl.BlockSpec((nh, LANES), lambda b, pp, pt: (0, 0)),
                  pl.BlockSpec((4, dh), lambda b, pp, pt: (0, 0)),
                  pl.BlockSpec((1, dv), lambda b, pp, pt: (0, 0))]
                 + [page_spec(i, 2 * dh) for i in range(pps)]
                 + [page_spec(i, dv) for i in range(pps)],
        out_specs=tok_spec(dv),
        scratch_shapes=[pltpu.VMEM((2 * nh, 1), F32),
                        pltpu.VMEM((2 * nh, 1), F32),
                        pltpu.VMEM((2 * nh, dv), F32),
                        pltpu.VMEM((2 * nh, page * nh), F32),
                        pltpu.VMEM((2 * nh, 2 * dh), F32)],
    )
    return pl.pallas_call(
        kern,
        grid_spec=grid_spec,
        out_shape=jax.ShapeDtypeStruct((bsz, nh, dv), F32),
        compiler_params=_params("parallel", "arbitrary"),
        name="attn_step",
    )(page_table.reshape(-1), q, kn, vn, slopes_rep, lamv, subln_g,
      *([cache_k] * pps), *([cache_v] * pps))


def kernel(x_prompt, x_sample, state_mlstm_c, state_mlstm_n, state_mlstm_m, state_conv, cache_k, cache_v, page_table, w_in_even, b_gate_even, conv_w_even, w_out_even, w_qkv_odd, w_o_odd, lambda_q1, lambda_k1, lambda_q2, lambda_k2, subln_g, w_ffn_gate, w_ffn_up, w_ffn_down, ln_mix_g, ln_mix_b, ln_ffn_g, ln_ffn_b):
    bp, t, d = x_prompt.shape
    bs = x_sample.shape[0]
    assert x_sample.shape[1] == 1, "the sample group decodes one token per sequence"
    depth = w_ffn_gate.shape[0]
    nh_a, dk_a, dv_a = state_mlstm_c.shape[2:]
    dm = nh_a * dk_a
    dc = state_conv.shape[-1]
    nh_c, dv_c = cache_v.shape[3:]
    dh_c = cache_k.shape[4] // 2
    alpha = (2 * depth) ** 0.25
    chunk = math.gcd(t, 256)

    xp = x_prompt.reshape(bp * t, d)
    xs = x_sample.reshape(bs, d)
    xp_b = xs_b = None
    slopes = jnp.exp2(-8.0 * jnp.arange(1, nh_c + 1, dtype=F32) / nh_c)
    slopes_rep = jnp.broadcast_to((slopes * LOG2E)[:, None], (nh_c, LANES))

    wg_all, wu_all, wd_all = w_ffn_gate.astype(BF16), w_ffn_up.astype(BF16), w_ffn_down.astype(BF16)

    pc, pn, pm, pconv, pk, pv = [], [], [], [], [], []
    sc, sn, sm, sconv, sk, sv = [], [], [], [], [], []
    for layer in range(depth):
        j = layer // 2
        g_mix, b_mix = ln_mix_g[layer][None], ln_mix_b[layer][None]
        g_ffn, b_ffn = ln_ffn_g[layer][None], ln_ffn_b[layer][None]
        if layer % 2 == 0:
            w_in = w_in_even[j]
            w_qkv = w_in[:, :3 * dm].astype(BF16)
            gate_lo = 4 * dm
            w_gate = jnp.pad(w_in[:, gate_lo:gate_lo + 2 * nh_a], ((0, 0), (0, LANES - 2 * nh_a))).astype(BF16)
            w_rest = jnp.concatenate([w_in[:, 3 * dm:4 * dm], w_in[:, gate_lo + 2 * nh_a:]], axis=1).astype(BF16)
            w_out = w_out_even[j].astype(BF16)
            b_gate_pad = jnp.pad(b_gate_even[j], (0, LANES - 2 * nh_a))[None]
            conv_w = conv_w_even[j]

            x_in = xp if xp_b is None else xp_b
            (qkv,) = _mm(x_in, w_qkv, [BF16], tn_pref=3 * dm // 2)
            rest, gates = _mm(x_in, w_rest, [F32], w_side=w_gate)
            mix, c1, n1, m1, cv1 = _evenmix_prompt(qkv, rest, gates, b_gate_pad, conv_w,
                                                   bp, t, nh_a, dk_a, dv_a, dc, chunk)
            xp, xp_b = _mm_ln(mix, w_out, xp, g_mix, b_mix, alpha)
            pc.append(c1); pn.append(n1.reshape(bp, nh_a, dk_a)); pm.append(m1[:, 0, :nh_a]); pconv.append(cv1)

            x_in = xs if xs_b is None else xs_b
            (qkv,) = _mm(x_in, w_qkv, [F32])
            rest, gates = _mm(x_in, w_rest, [F32], w_side=w_gate)
            conv_in = rest[:, dm:].reshape(bs, 3, dc)
            m0_pad = jnp.pad(state_mlstm_m[j], ((0, 0), (0, LANES - nh_a)))[:, None, :]
            mix, c2, n2, m2, cv2 = _evenmix_step(
                qkv[:, :dm], qkv[:, dm:2 * dm], qkv[:, 2 * dm:], rest[:, :dm], conv_in,
                gates[:, None, :], b_gate_pad, conv_w, state_mlstm_c[j],
                state_mlstm_n[j][:, :, None, :], m0_pad, state_conv[j], nh_a, dk_a, dv_a, dc)
            xs, xs_b = _mm_ln(mix.reshape(bs, dm + dc).astype(BF16), w_out, xs, g_mix, b_mix, alpha)
            sc.append(c2); sn.append(n2.reshape(bs, nh_a, dk_a)); sm.append(m2[:, 0, :nh_a]); sconv.append(cv2)
        else:
            lam_init = 0.8 - 0.6 * math.exp(-0.3 * layer)
            d_attn = nh_c * dv_c
            w_qkv = w_qkv_odd[j].astype(BF16)
            w_o = w_o_odd[j].astype(BF16)
            lamv = jnp.stack([lambda_q1[j], lambda_k1[j], lambda_q2[j], lambda_k2[j]])
            sg = subln_g[j][None]
            q_scale = dh_c ** -0.5 * LOG2E

            x_in = xp if xp_b is None else xp_b
            (q_b,) = _mm(x_in, w_qkv, [BF16], scale=q_scale, col0=0, n=d_attn, tn_pref=2048)
            (k_f,) = _mm(x_in, w_qkv, [F32], col0=d_attn, n=d_attn, tn_pref=2048)
            (v_f,) = _mm(x_in, w_qkv, [F32], col0=2 * d_attn, n=d_attn, tn_pref=2048)
            att = _attn_prompt(q_b, k_f, v_f, slopes, lamv, sg, bp, t, nh_c, dh_c, dv_c, lam_init)
            xp, xp_b = _mm_ln(att, w_o, xp, g_mix, b_mix, alpha)
            pk.append(k_f.reshape(bp, t, nh_c, 2 * dh_c)); pv.append(v_f.reshape(bp, t, nh_c, dv_c))

            x_in = xs if xs_b is None else xs_b
            (qkv_s,) = _mm(x_in, w_qkv, [F32], tn_pref=2048)
            q_s, k_s, v_s = qkv_s[:, :d_attn], qkv_s[:, d_attn:2 * d_attn], qkv_s[:, 2 * d_attn:]
            att = _attn_step(q_s.reshape(bs, nh_c, 2 * dh_c), k_s.reshape(bs, nh_c, 2 * dh_c),
                             v_s.reshape(bs, nh_c, dv_c), cache_k, cache_v, j, page_table,
                             slopes_rep, lamv, sg, nh_c, dh_c, dv_c, lam_init, q_scale)
            xs, xs_b = _mm_ln(att.reshape(bs, d_attn).astype(BF16), w_o, xs, g_mix, b_mix, alpha)
            sk.append(k_s.reshape(bs, 1, nh_c, 2 * dh_c)); sv.append(v_s.reshape(bs, 1, nh_c, dv_c))

        xp, xp_b = _ffn(xp_b, wg_all, wu_all, wd_all, layer, xp, g_ffn, b_ffn, alpha)
        xs, xs_b = _ffn(xs_b, wg_all, wu_all, wd_all, layer, xs, g_ffn, b_ffn, alpha)

    return (xp.reshape(bp, t, d), xs.reshape(bs, 1, d),
            jnp.stack(pc), jnp.stack(pn), jnp.stack(pm), jnp.stack(pconv), jnp.stack(pk), jnp.stack(pv),
            jnp.stack(sc), jnp.stack(sn), jnp.stack(sm), jnp.stack(sconv), jnp.stack(sk), jnp.stack(sv))
```
